```python
import math
import jax, jax.numpy as jnp
from jax import lax
import numpy as np

D_MODEL = 1024
BATCH = 8
SEQ = 4096
DEPTH = 4

MIX_WIDTH = D_MODEL
ATTN_HEADS = 8
ATTN_HEAD_DIM = 64
ATTN_DIM = ATTN_HEADS * ATTN_HEAD_DIM
CONV_DIM = D_MODEL // 4
CONV_WIDTH = 3
POOL_WINDOWS = (2, 4, 8, 16)
N_POOL_GROUPS = len(POOL_WINDOWS)
POOL_DIM = D_MODEL // 4
POOL_GD = POOL_DIM // N_POOL_GROUPS
D_FF = 2816
Q_BLOCK = 128
RMS_EPS = 1e-6
SPLIT_SIZES = (ATTN_DIM, ATTN_DIM, ATTN_DIM, CONV_DIM, CONV_DIM, CONV_DIM, POOL_DIM)
IN_PROJ_WIDTH = sum(SPLIT_SIZES)
SPLIT_POINTS = tuple(int(v) for v in np.cumsum(SPLIT_SIZES)[:-1])

kernel_name = "hybrid_sbattn_shortconv_pool_macaron"


def _rmsnorm(x, g):
    x32 = x.astype(jnp.float32)
    y = x32 * lax.rsqrt(jnp.mean(x32 * x32, axis=-1, keepdims=True) + RMS_EPS)
    return (y * g.astype(jnp.float32)).astype(x.dtype)


def _swiglu(h, w_gate, w_up, w_down):
    return (jax.nn.silu(h @ w_gate) * (h @ w_up)) @ w_down


def _stick_breaking_attention(q, k, v):
    B, S, H, dh = q.shape
    nb = S // Q_BLOCK
    scale = 1.0 / math.sqrt(dh)
    kh = k.transpose(0, 2, 1, 3)
    vh = v.transpose(0, 2, 1, 3)
    q_blocks = q.transpose(0, 2, 1, 3).reshape(B, H, nb, Q_BLOCK, dh).transpose(2, 0, 1, 3, 4)
    k_pos = jnp.arange(S)

    def block(args):
        q_blk, start = args
        q_pos = start + jnp.arange(Q_BLOCK)
        z = jnp.einsum('bhqd,bhkd->bhqk', q_blk, kh).astype(jnp.float32) * scale
        mask = k_pos[None, :] < q_pos[:, None]
        log_keep = jnp.where(mask, jax.nn.log_sigmoid(-z), 0.0)
        between = lax.cumsum(log_keep, axis=3, reverse=True) - log_keep
        a = jnp.where(mask, jnp.exp(jax.nn.log_sigmoid(z) + between), 0.0)
        return jnp.einsum('bhqk,bhkd->bhqd', a.astype(vh.dtype), vh)

    out = lax.map(block, (q_blocks, jnp.arange(nb) * Q_BLOCK))
    return out.transpose(1, 0, 3, 2, 4).reshape(B, S, H * dh)


def _short_conv(gate_b, gate_c, h, conv_w):
    S = h.shape[1]
    u = gate_c * h
    u_pad = jnp.pad(u, ((0, 0), (CONV_WIDTH - 1, 0), (0, 0)))
    y = conv_w[0] * u_pad[:, 0:S] + conv_w[1] * u_pad[:, 1:S + 1] + conv_w[2] * u_pad[:, 2:S + 2]
    return gate_b * y


def _multiscale_pool(p, pool_w, pool_scale):
    B, S, C = p.shape
    p32 = p.astype(jnp.float32).reshape(B, S, N_POOL_GROUPS, POOL_GD)
    cs0 = jnp.pad(jnp.cumsum(p32, axis=1), ((0, 0), (1, 0), (0, 0), (0, 0)))
    count_base = jnp.arange(1, S + 1, dtype=jnp.float32)
    outs = []
    for gi, w in enumerate(POOL_WINDOWS):
        lagged = jnp.pad(cs0[:, :S + 1 - w, gi], ((0, 0), (w - 1, 0), (0, 0)))
        mean = (cs0[:, 1:, gi] - lagged) / jnp.minimum(count_base, w)[None, :, None]
        outs.append(mean - p32[:, :, gi])
    d = jnp.stack(outs, axis=2).astype(p.dtype)
    y = jnp.einsum('bsgc,gcd->bsgd', d, pool_w).reshape(B, S, C)
    return y * pool_scale


def _mixer(h, w_in, conv_w, pool_w, pool_scale, w_out):
    B, S, _ = h.shape
    proj = h @ w_in
    q, k, v, gate_b, gate_c, conv_h, pool_in = jnp.split(proj, SPLIT_POINTS, axis=-1)
    q = q.reshape(B, S, ATTN_HEADS, ATTN_HEAD_DIM)
    k = k.reshape(B, S, ATTN_HEADS, ATTN_HEAD_DIM)
    v = v.reshape(B, S, ATTN_HEADS, ATTN_HEAD_DIM)
    attn = _stick_breaking_attention(q, k, v)
    conv = _short_conv(gate_b, gate_c, conv_h, conv_w)
    pool = _multiscale_pool(pool_in, pool_w, pool_scale)
    return jnp.concatenate([attn, conv, pool], axis=-1) @ w_out


def setup_inputs(seed: int = 0) -> dict:
    key = jax.random.key(seed)
    ks = jax.random.split(key, 11)
    f32 = jnp.float32
    x = jax.random.normal(ks[0], (BATCH, SEQ, D_MODEL), f32)
    norm_g = 1.0 + 0.05 * jax.random.normal(ks[1], (DEPTH, 6, D_MODEL), f32)
    ffn_w_gate = jax.random.normal(ks[2], (DEPTH, 2, D_MODEL, D_FF), f32) * D_MODEL ** -0.5
    ffn_w_up = jax.random.normal(ks[3], (DEPTH, 2, D_MODEL, D_FF), f32) * D_MODEL ** -0.5
    ffn_w_down = jax.random.normal(ks[4], (DEPTH, 2, D_FF, D_MODEL), f32) * D_FF ** -0.5
    w_in = jax.random.normal(ks[5], (DEPTH, D_MODEL, IN_PROJ_WIDTH), f32) * D_MODEL ** -0.5
    conv_w = jax.random.normal(ks[6], (DEPTH, CONV_WIDTH, CONV_DIM), f32) * CONV_WIDTH ** -0.5
    pool_w = jax.random.normal(ks[7], (DEPTH, N_POOL_GROUPS, POOL_GD, POOL_GD), f32) * POOL_GD ** -0.5
    pool_scale = 1.0 + 0.05 * jax.random.normal(ks[8], (DEPTH, POOL_DIM), f32)
    w_out = jax.random.normal(ks[9], (DEPTH, MIX_WIDTH, D_MODEL), f32) * MIX_WIDTH ** -0.5
    return {"x": x, "norm_g": norm_g, "ffn_w_gate": ffn_w_gate, "ffn_w_up": ffn_w_up,
            "ffn_w_down": ffn_w_down, "w_in": w_in, "conv_w": conv_w, "pool_w": pool_w,
            "pool_scale": pool_scale, "w_out": w_out}


def reference(x, norm_g, ffn_w_gate, ffn_w_up, ffn_w_down, w_in, conv_w, pool_w, pool_scale, w_out):
    for l in range(DEPTH):
        g = norm_g[l]
        f = _swiglu(_rmsnorm(x, g[0]), ffn_w_gate[l, 0], ffn_w_up[l, 0], ffn_w_down[l, 0])
        x = x + 0.5 * _rmsnorm(f, g[1])
        m = _mixer(_rmsnorm(x, g[2]), w_in[l], conv_w[l], pool_w[l], pool_scale[l], w_out[l])
        x = x + _rmsnorm(m, g[3])
        f = _swiglu(_rmsnorm(x, g[4]), ffn_w_gate[l, 1], ffn_w_up[l, 1], ffn_w_down[l, 1])
        x = x + 0.5 * _rmsnorm(f, g[5])
    return x
```

```python
import functools
import math

import jax
import jax.numpy as jnp
from jax import lax
from jax.experimental import pallas as pl
from jax.experimental.pallas import tpu as pltpu

F32 = jnp.float32
BF16 = jnp.bfloat16

D_MODEL = 1024
D_FF = 2816
ATTN_HEADS = 8
ATTN_HEAD_DIM = 64
ATTN_DIM = ATTN_HEADS * ATTN_HEAD_DIM
CONV_DIM = 256
CONV_WIDTH = 3
POOL_WINDOWS = (2, 4, 8, 16)
POOL_DIM = 256
POOL_GD = POOL_DIM // len(POOL_WINDOWS)
IN_PROJ_WIDTH = 3 * ATTN_DIM + 3 * CONV_DIM + POOL_DIM
RMS_EPS = 1e-6

LANES = 128
FF_CHUNK = 256
N_FF_CHUNKS = D_FF // FF_CHUNK
FFN_ROWS = 512
PROJ_ROWS = 512
HALO = 32
ATT_Q = 256
ATT_K = 256
VMEM_LIMIT = 56 * 1024 * 1024


def _rmsnorm(x, g):
    ms = jnp.mean(x * x, axis=-1, keepdims=True)
    return x * lax.rsqrt(ms + RMS_EPS) * g


def _ffn_kernel(x_ref, gpre_ref, gpost_ref, wg_ref, wu_ref, wd_ref, o_ref):
    x = x_ref[...]
    h = _rmsnorm(x, gpre_ref[...]).astype(BF16)
    acc = jnp.zeros(x.shape, F32)
    for c in range(N_FF_CHUNKS):
        g = jnp.dot(h, wg_ref[c], preferred_element_type=F32)
        u = jnp.dot(h, wu_ref[c], preferred_element_type=F32)
        a = g * jax.nn.sigmoid(g) * u
        acc = acc + jnp.dot(a.astype(BF16), wd_ref[c], preferred_element_type=F32)
    o_ref[...] = x + 0.5 * _rmsnorm(acc, gpost_ref[...])


def _ffn(x2d, g_pre, g_post, wg, wu, wd):
    n = x2d.shape[0]
    row_spec = pl.BlockSpec((FFN_ROWS, D_MODEL), lambda i: (i, 0))
    gain_spec = pl.BlockSpec((1, D_MODEL), lambda i: (0, 0))
    return pl.pallas_call(
        _ffn_kernel,
        name="ffn",
        grid=(n // FFN_ROWS,),
        in_specs=[
            row_spec, gain_spec, gain_spec,
            pl.BlockSpec((N_FF_CHUNKS, D_MODEL, FF_CHUNK), lambda i: (0, 0, 0)),
            pl.BlockSpec((N_FF_CHUNKS, D_MODEL, FF_CHUNK), lambda i: (0, 0, 0)),
            pl.BlockSpec((N_FF_CHUNKS, FF_CHUNK, D_MODEL), lambda i: (0, 0, 0)),
        ],
        out_specs=row_spec,
        out_shape=jax.ShapeDtypeStruct(x2d.shape, F32),
        compiler_params=pltpu.CompilerParams(
            dimension_semantics=("arbitrary",), vmem_limit_bytes=VMEM_LIMIT),
    )(x2d, g_pre, g_post, wg, wu, wd)


def _in_proj_kernel(tiles_per_seq, x_ref, g_ref, w_ref, convw_ref, poolw_ref, pscale_ref,
                    q_ref, k_ref, v_ref, cp_ref, ubuf, pbuf, s2buf, s4buf, s8buf):
    rows = PROJ_ROWS
    tile_in_seq = pl.program_id(0) % tiles_per_seq

    @pl.when(tile_in_seq == 0)
    def _():
        ubuf[0:HALO, :] = jnp.zeros((HALO, CONV_DIM), F32)
        pbuf[0:HALO, :] = jnp.zeros((HALO, POOL_DIM), F32)

    @pl.when(tile_in_seq != 0)
    def _():
        ubuf[0:HALO, :] = ubuf[rows:rows + HALO, :]
        pbuf[0:HALO, :] = pbuf[rows:rows + HALO, :]

    h = _rmsnorm(x_ref[...], g_ref[...]).astype(BF16)
    proj = jnp.dot(h, w_ref[...], preferred_element_type=F32)
    a = ATTN_DIM
    q_ref[...] = (proj[:, 0:a] * (1.0 / math.sqrt(ATTN_HEAD_DIM))).astype(BF16)
    k_ref[...] = proj[:, a:2 * a].astype(BF16)
    v_ref[...] = proj[:, 2 * a:3 * a].astype(BF16)
    c0 = 3 * a
    gate_b = proj[:, c0:c0 + CONV_DIM]
    gate_c = proj[:, c0 + CONV_DIM:c0 + 2 * CONV_DIM]
    conv_h = proj[:, c0 + 2 * CONV_DIM:c0 + 3 * CONV_DIM]
    p = proj[:, c0 + 3 * CONV_DIM:]

    ubuf[HALO:HALO + rows, :] = gate_c * conv_h
    cw = convw_ref[...]
    y = (cw[0:1, :] * ubuf[HALO - 2:HALO - 2 + rows, :]
         + cw[1:2, :] * ubuf[HALO - 1:HALO - 1 + rows, :]
         + cw[2:3, :] * ubuf[HALO:HALO + rows, :])
    cp_ref[:, 0:CONV_DIM] = (gate_b * y).astype(BF16)

    pbuf[HALO:HALO + rows, :] = p
    end = HALO + rows
    s2buf[8:end, :] = pbuf[8:end, :] + pbuf[7:end - 1, :]
    s4buf[16:end, :] = s2buf[16:end, :] + s2buf[14:end - 2, :]
    s8buf[24:end, :] = s4buf[24:end, :] + s4buf[20:end - 4, :]
    s16 = s8buf[HALO:end, :] + s8buf[HALO - 8:end - 8, :]
    group = lax.broadcasted_iota(jnp.int32, (rows, POOL_DIM), 1) // POOL_GD
    win_sum = jnp.where(group == 0, s2buf[HALO:end, :],
                        jnp.where(group == 1, s4buf[HALO:end, :],
                                  jnp.where(group == 2, s8buf[HALO:end, :], s16)))
    window = jnp.where(group == 0, POOL_WINDOWS[0],
                       jnp.where(group == 1, POOL_WINDOWS[1],
                                 jnp.where(group == 2, POOL_WINDOWS[2], POOL_WINDOWS[3])))
    pos = tile_in_seq * rows + lax.broadcasted_iota(jnp.int32, (rows, POOL_DIM), 0)
    count = jnp.minimum(pos + 1, window).astype(F32)
    d = win_sum / count - p
    pooled = jnp.dot(d.astype(BF16), poolw_ref[...], preferred_element_type=F32)
    cp_ref[:, CONV_DIM:] = (pooled * pscale_ref[...]).astype(BF16)


def _in_proj(x2d, seq, g, w_in, conv_w, pool_w_bd, pool_scale):
    n = x2d.shape[0]
    tiles_per_seq = seq // PROJ_ROWS
    row = lambda width: pl.BlockSpec((PROJ_ROWS, width), lambda i: (i, 0))
    full = lambda shape: pl.BlockSpec(shape, lambda i: (0,) * len(shape))
    att = jax.ShapeDtypeStruct((n, ATTN_DIM), BF16)
    buf = lambda width: pltpu.VMEM((HALO + PROJ_ROWS, width), F32)
    return pl.pallas_call(
        functools.partial(_in_proj_kernel, tiles_per_seq),
        name="in_proj",
        grid=(n // PROJ_ROWS,),
        in_specs=[
            row(D_MODEL), full((1, D_MODEL)), full((D_MODEL, IN_PROJ_WIDTH)),
            full((CONV_WIDTH, CONV_DIM)), full((POOL_DIM, POOL_DIM)), full((1, POOL_DIM)),
        ],
        out_specs=[row(ATTN_DIM), row(ATTN_DIM), row(ATTN_DIM), row(CONV_DIM + POOL_DIM)],
        out_shape=[att, att, att, jax.ShapeDtypeStruct((n, CONV_DIM + POOL_DIM), BF16)],
        scratch_shapes=[buf(CONV_DIM), buf(POOL_DIM), buf(POOL_DIM), buf(POOL_DIM), buf(POOL_DIM)],
        compiler_params=pltpu.CompilerParams(
            dimension_semantics=("arbitrary",), vmem_limit_bytes=VMEM_LIMIT),
    )(x2d, g, w_in, conv_w, pool_w_bd, pool_scale)


def _attn_block(qm, k2, v2, carry, tri, mask):
    z = lax.dot_general(qm, k2, (((1,), (1,)), ((), ())), preferred_element_type=F32)
    soft = jnp.log(1.0 + jnp.exp(-jnp.abs(z)))
    log_keep = -(jnp.maximum(z, 0.0) + soft)
    log_beta = jnp.minimum(z, 0.0) - soft
    if mask is not None:
        log_keep = jnp.where(mask, log_keep, 0.0)
    lk = log_keep.astype(BF16)
    between = jnp.dot(lk, tri, preferred_element_type=F32)
    a = jnp.exp(log_beta + between + carry)
    if mask is not None:
        a = jnp.where(mask, a, 0.0)
    out = jnp.dot(a.astype(BF16), v2, preferred_element_type=F32)
    new_carry = carry + between[:, 0:1] + log_keep[:, 0:1]
    return out, new_carry


def _attn_out_kernel(x_ref, q_ref, k_ref, v_ref, cp_ref, wout_ref, g_ref, o_ref,
                     mix_ref, acc_ref, carry_ref):
    qi = pl.program_id(1)
    row = lax.broadcasted_iota(jnp.int32, (ATT_Q, ATT_K), 0)
    col = lax.broadcasted_iota(jnp.int32, (ATT_Q, ATT_K), 1)
    causal = col < row
    tri = jnp.where(row > col, 1.0, 0.0).astype(BF16)
    lane = lax.broadcasted_iota(jnp.int32, (ATT_Q, LANES), 1)
    first_head = lane < ATTN_HEAD_DIM

    for hp in range(ATTN_HEADS // 2):
        lanes = slice(hp * LANES, (hp + 1) * LANES)
        q2 = q_ref[0, :, lanes]
        q_heads = (jnp.where(first_head, q2, jnp.zeros_like(q2)),
                   jnp.where(first_head, jnp.zeros_like(q2), q2))

        diag = pl.multiple_of(qi * ATT_K, ATT_K)
        k2 = k_ref[0, pl.ds(diag, ATT_K), lanes]
        v2 = v_ref[0, pl.ds(diag, ATT_K), lanes]
        for hh in range(2):
            out, carry = _attn_block(q_heads[hh], k2, v2, jnp.zeros((ATT_Q, 1), F32), tri, causal)
            acc_ref[hh] = out
            carry_ref[hh] = carry

        def body(step, _):
            start = pl.multiple_of((qi - 1 - step) * ATT_K, ATT_K)
            kb = k_ref[0, pl.ds(start, ATT_K), lanes]
            vb = v_ref[0, pl.ds(start, ATT_K), lanes]
            for hh in range(2):
                out, carry = _attn_block(q_heads[hh], kb, vb, carry_ref[hh], tri, None)
                acc_ref[hh] += out
                carry_ref[hh] = carry
            return 0

        lax.fori_loop(0, qi, body, 0)
        mix_ref[:, lanes] = jnp.where(first_head, acc_ref[0], acc_ref[1]).astype(BF16)

    mix_ref[:, ATTN_DIM:] = cp_ref[0]
    m = jnp.dot(mix_ref[...], wout_ref[...], preferred_element_type=F32)
    o_ref[0] = x_ref[0] + _rmsnorm(m, g_ref[...])


def _attn_out(x3d, q, k, v, cp, w_out, g):
    b, s, _ = x3d.shape
    tile = lambda width: pl.BlockSpec((1, ATT_Q, width), lambda bi, qi: (bi, qi, 0))
    seq = lambda width: pl.BlockSpec((1, s, width), lambda bi, qi: (bi, 0, 0))
    return pl.pallas_call(
        _attn_out_kernel,
        name="attn_out",
        grid=(b, s // ATT_Q),
        in_specs=[
            tile(D_MODEL), tile(ATTN_DIM), seq(ATTN_DIM), seq(ATTN_DIM), tile(CONV_DIM + POOL_DIM),
            pl.BlockSpec((D_MODEL, D_MODEL), lambda bi, qi: (0, 0)),
            pl.BlockSpec((1, D_MODEL), lambda bi, qi: (0, 0)),
        ],
        out_specs=tile(D_MODEL),
        out_shape=jax.ShapeDtypeStruct(x3d.shape, F32),
        scratch_shapes=[
            pltpu.VMEM((ATT_Q, D_MODEL), BF16),
            pltpu.VMEM((2, ATT_Q, LANES), F32),
            pltpu.VMEM((2, ATT_Q, 1), F32),
        ],
        compiler_params=pltpu.CompilerParams(
            dimension_semantics=("arbitrary", "arbitrary"), vmem_limit_bytes=VMEM_LIMIT),
    )(x3d, q, k, v, cp, w_out, g)


def _ffn_weights(w_gate, w_up, w_down):
    cols = lambda w: w.astype(BF16).reshape(D_MODEL, N_FF_CHUNKS, FF_CHUNK).transpose(1, 0, 2)
    return cols(w_gate), cols(w_up), w_down.astype(BF16).reshape(N_FF_CHUNKS, FF_CHUNK, D_MODEL)


def _block_diag(pool_w):
    out = jnp.zeros((POOL_DIM, POOL_DIM), pool_w.dtype)
    for gi in range(len(POOL_WINDOWS)):
        sl = slice(gi * POOL_GD, (gi + 1) * POOL_GD)
        out = out.at[sl, sl].set(pool_w[gi])
    return out


def kernel(x, norm_g, ffn_w_gate, ffn_w_up, ffn_w_down, w_in, conv_w, pool_w, pool_scale, w_out):
    b, s, d = x.shape
    assert d == D_MODEL and s % PROJ_ROWS == 0 and s % ATT_Q == 0 and (b * s) % FFN_ROWS == 0
    depth = norm_g.shape[0]
    x2d = x.reshape(b * s, d)
    for l in range(depth):
        g = norm_g[l].reshape(6, 1, D_MODEL)
        x2d = _ffn(x2d, g[0], g[1], *_ffn_weights(ffn_w_gate[l, 0], ffn_w_up[l, 0], ffn_w_down[l, 0]))
        q, k, v, cp = _in_proj(x2d, s, g[2], w_in[l].astype(BF16), conv_w[l],
                               _block_diag(pool_w[l]).astype(BF16), pool_scale[l].reshape(1, POOL_DIM))
        r3 = lambda t: t.reshape(b, s, t.shape[-1])
        x2d = _attn_out(r3(x2d), r3(q), r3(k), r3(v), r3(cp), w_out[l].astype(BF16), g[3]).reshape(b * s, d)
        x2d = _ffn(x2d, g[4], g[5], *_ffn_weights(ffn_w_gate[l, 1], ffn_w_up[l, 1], ffn_w_down[l, 1]))
    return x2d.reshape(b, s, d)
```

```python
import functools
import math

import jax
import jax.numpy as jnp
from jax import lax
from jax.experimental import pallas as pl
from jax.experimental.pallas import tpu as pltpu

F32 = jnp.float32
BF16 = jnp.bfloat16

D_MODEL = 1024
D_FF = 2816
ATTN_HEADS = 8
ATTN_HEAD_DIM = 64
ATTN_DIM = ATTN_HEADS * ATTN_HEAD_DIM
CONV_DIM = 256
CONV_WIDTH = 3
POOL_WINDOWS = (2, 4, 8, 16)
POOL_DIM = 256
POOL_GD = POOL_DIM // len(POOL_WINDOWS)
IN_PROJ_WIDTH = 3 * ATTN_DIM + 3 * CONV_DIM + POOL_DIM
RMS_EPS = 1e-6

LANES = 128
FF_CHUNK = 256
N_FF_CHUNKS = D_FF // FF_CHUNK
FFN_ROWS = 512
PROJ_ROWS = 512
HALO = 32
ATT_Q = 256
ATT_K = 256
VMEM_LIMIT = 56 * 1024 * 1024


def _rmsnorm(x, g):
    ms = jnp.mean(x * x, axis=-1, keepdims=True)
    return x * lax.rsqrt(ms + RMS_EPS) * g


def _ffn_kernel(x_ref, gpre_ref, gpost_ref, wg_ref, wu_ref, wd_ref, o_ref):
    x = x_ref[...]
    h = _rmsnorm(x, gpre_ref[...]).astype(BF16)
    acc = jnp.zeros(x.shape, F32)
    for c in range(N_FF_CHUNKS):
        g = jnp.dot(h, wg_ref[c], preferred_element_type=F32)
        u = jnp.dot(h, wu_ref[c], preferred_element_type=F32)
        a = g * jax.nn.sigmoid(g) * u
        acc = acc + jnp.dot(a.astype(BF16), wd_ref[c], preferred_element_type=F32)
    o_ref[...] = x + 0.5 * _rmsnorm(acc, gpost_ref[...])


def _ffn(x2d, g_pre, g_post, wg, wu, wd):
    n = x2d.shape[0]
    row_spec = pl.BlockSpec((FFN_ROWS, D_MODEL), lambda i: (i, 0))
    gain_spec = pl.BlockSpec((1, D_MODEL), lambda i: (0, 0))
    return pl.pallas_call(
        _ffn_kernel,
        name="ffn",
        grid=(n // FFN_ROWS,),
        in_specs=[
            row_spec, gain_spec, gain_spec,
            pl.BlockSpec((N_FF_CHUNKS, D_MODEL, FF_CHUNK), lambda i: (0, 0, 0)),
            pl.BlockSpec((N_FF_CHUNKS, D_MODEL, FF_CHUNK), lambda i: (0, 0, 0)),
            pl.BlockSpec((N_FF_CHUNKS, FF_CHUNK, D_MODEL), lambda i: (0, 0, 0)),
        ],
        out_specs=row_spec,
        out_shape=jax.ShapeDtypeStruct(x2d.shape, F32),
        compiler_params=pltpu.CompilerParams(
            dimension_semantics=("arbitrary",), vmem_limit_bytes=VMEM_LIMIT),
    )(x2d, g_pre, g_post, wg, wu, wd)


def _in_proj_kernel(tiles_per_seq, x_ref, g_ref, w_ref, convw_ref, poolw_ref, pscale_ref,
                    q_ref, k_ref, v_ref, cp_ref, ubuf, pbuf, s2buf, s4buf, s8buf):
    rows = PROJ_ROWS
    tile_in_seq = pl.program_id(0) % tiles_per_seq

    @pl.when(tile_in_seq == 0)
    def _():
        ubuf[0:HALO, :] = jnp.zeros((HALO, CONV_DIM), F32)
        pbuf[0:HALO, :] = jnp.zeros((HALO, POOL_DIM), F32)

    @pl.when(tile_in_seq != 0)
    def _():
        ubuf[0:HALO, :] = ubuf[rows:rows + HALO, :]
        pbuf[0:HALO, :] = pbuf[rows:rows + HALO, :]

    h = _rmsnorm(x_ref[...], g_ref[...]).astype(BF16)
    proj = jnp.dot(h, w_ref[...], preferred_element_type=F32)
    a = ATTN_DIM
    q_ref[...] = (proj[:, 0:a] * (1.0 / math.sqrt(ATTN_HEAD_DIM))).astype(BF16)
    k_ref[...] = proj[:, a:2 * a].astype(BF16)
    v_ref[...] = proj[:, 2 * a:3 * a].astype(BF16)
    c0 = 3 * a
    gate_b = proj[:, c0:c0 + CONV_DIM]
    gate_c = proj[:, c0 + CONV_DIM:c0 + 2 * CONV_DIM]
    conv_h = proj[:, c0 + 2 * CONV_DIM:c0 + 3 * CONV_DIM]
    p = proj[:, c0 + 3 * CONV_DIM:]

    ubuf[HALO:HALO + rows, :] = gate_c * conv_h
    cw = convw_ref[...]
    y = (cw[0:1, :] * ubuf[HALO - 2:HALO - 2 + rows, :]
         + cw[1:2, :] * ubuf[HALO - 1:HALO - 1 + rows, :]
         + cw[2:3, :] * ubuf[HALO:HALO + rows, :])
    cp_ref[:, 0:CONV_DIM] = (gate_b * y).astype(BF16)

    pbuf[HALO:HALO + rows, :] = p
    end = HALO + rows
    s2buf[8:end, :] = pbuf[8:end, :] + pbuf[7:end - 1, :]
    s4buf[16:end, :] = s2buf[16:end, :] + s2buf[14:end - 2, :]
    s8buf[24:end, :] = s4buf[24:end, :] + s4buf[20:end - 4, :]
    s16 = s8buf[HALO:end, :] + s8buf[HALO - 8:end - 8, :]
    group = lax.broadcasted_iota(jnp.int32, (rows, POOL_DIM), 1) // POOL_GD
    win_sum = jnp.where(group == 0, s2buf[HALO:end, :],
                        jnp.where(group == 1, s4buf[HALO:end, :],
                                  jnp.where(group == 2, s8buf[HALO:end, :], s16)))
    window = jnp.where(group == 0, POOL_WINDOWS[0],
                       jnp.where(group == 1, POOL_WINDOWS[1],
                                 jnp.where(group == 2, POOL_WINDOWS[2], POOL_WINDOWS[3])))
    pos = tile_in_seq * rows + lax.broadcasted_iota(jnp.int32, (rows, POOL_DIM), 0)
    count = jnp.minimum(pos + 1, window).astype(F32)
    d = win_sum / count - p
    pooled = jnp.dot(d.astype(BF16), poolw_ref[...], preferred_element_type=F32)
    cp_ref[:, CONV_DIM:] = (pooled * pscale_ref[...]).astype(BF16)


def _in_proj(x2d, seq, g, w_in, conv_w, pool_w_bd, pool_scale):
    n = x2d.shape[0]
    tiles_per_seq = seq // PROJ_ROWS
    row = lambda width: pl.BlockSpec((PROJ_ROWS, width), lambda i: (i, 0))
    full = lambda shape: pl.BlockSpec(shape, lambda i: (0,) * len(shape))
    att = jax.ShapeDtypeStruct((n, ATTN_DIM), BF16)
    buf = lambda width: pltpu.VMEM((HALO + PROJ_ROWS, width), F32)
    return pl.pallas_call(
        functools.partial(_in_proj_kernel, tiles_per_seq),
        name="in_proj",
        grid=(n // PROJ_ROWS,),
        in_specs=[
            row(D_MODEL), full((1, D_MODEL)), full((D_MODEL, IN_PROJ_WIDTH)),
            full((CONV_WIDTH, CONV_DIM)), full((POOL_DIM, POOL_DIM)), full((1, POOL_DIM)),
        ],
        out_specs=[row(ATTN_DIM), row(ATTN_DIM), row(ATTN_DIM), row(CONV_DIM + POOL_DIM)],
        out_shape=[att, att, att, jax.ShapeDtypeStruct((n, CONV_DIM + POOL_DIM), BF16)],
        scratch_shapes=[buf(CONV_DIM), buf(POOL_DIM), buf(POOL_DIM), buf(POOL_DIM), buf(POOL_DIM)],
        compiler_params=pltpu.CompilerParams(
            dimension_semantics=("arbitrary",), vmem_limit_bytes=VMEM_LIMIT),
    )(x2d, g, w_in, conv_w, pool_w_bd, pool_scale)


def _attn_block(qm, k2, v2, carry, neg_tri, mask):
    z = lax.dot_general(qm, k2, (((1,), (1,)), ((), ())), preferred_element_type=F32)
    neg_abs = lax.bitcast_convert_type(
        lax.bitcast_convert_type(z, jnp.uint32) | jnp.uint32(0x80000000), F32)
    soft = jnp.log(1.0 + jnp.exp(neg_abs))
    drop = jnp.maximum(z, 0.0) + soft
    log_beta = z - drop
    if mask is not None:
        drop = jnp.where(mask, drop, 0.0)
    between = jnp.dot(drop.astype(BF16), neg_tri, preferred_element_type=F32)
    a = jnp.exp(log_beta + between + carry)
    if mask is not None:
        a = jnp.where(mask, a, 0.0)
    out = jnp.dot(a.astype(BF16), v2, preferred_element_type=F32)
    new_carry = carry + between[:, 0:1] - drop[:, 0:1]
    return out, new_carry


def _attn_out_kernel(x_ref, q_ref, k_ref, v_ref, cp_ref, wout_ref, g_ref, o_ref,
                     qm_ref, mix_ref, acc_ref, carry_ref):
    qi = pl.program_id(1)
    n_pairs = ATTN_HEADS // 2
    row = lax.broadcasted_iota(jnp.int32, (ATT_Q, ATT_K), 0)
    col = lax.broadcasted_iota(jnp.int32, (ATT_Q, ATT_K), 1)
    causal = jnp.concatenate([col < row] * 2, axis=0)
    neg_tri = jnp.where(row > col, -1.0, 0.0).astype(BF16)
    lane = lax.broadcasted_iota(jnp.int32, (ATT_Q, LANES), 1)
    first_head = lane < ATTN_HEAD_DIM
    pair_lanes = [slice(hp * LANES, (hp + 1) * LANES) for hp in range(n_pairs)]

    for hp in range(n_pairs):
        q2 = q_ref[0, :, pair_lanes[hp]]
        qm_ref[hp, 0:ATT_Q, :] = jnp.where(first_head, q2, jnp.zeros_like(q2))
        qm_ref[hp, ATT_Q:, :] = jnp.where(first_head, jnp.zeros_like(q2), q2)

    def block(start, mask, is_first):
        for hp in range(n_pairs):
            k2 = k_ref[0, pl.ds(start, ATT_K), pair_lanes[hp]]
            v2 = v_ref[0, pl.ds(start, ATT_K), pair_lanes[hp]]
            carry = jnp.zeros((2 * ATT_Q, 1), F32) if is_first else carry_ref[hp]
            out, carry = _attn_block(qm_ref[hp], k2, v2, carry, neg_tri, mask)
            acc_ref[hp] = out if is_first else acc_ref[hp] + out
            carry_ref[hp] = carry

    block(pl.multiple_of(qi * ATT_K, ATT_K), causal, True)

    def body(step, _):
        block(pl.multiple_of((qi - 1 - step) * ATT_K, ATT_K), None, False)
        return 0

    lax.fori_loop(0, qi, body, 0)

    for hp in range(n_pairs):
        mix_ref[:, pair_lanes[hp]] = jnp.where(
            first_head, acc_ref[hp, 0:ATT_Q, :], acc_ref[hp, ATT_Q:, :]).astype(BF16)
    mix_ref[:, ATTN_DIM:] = cp_ref[0]
    m = jnp.dot(mix_ref[...], wout_ref[...], preferred_element_type=F32)
    o_ref[0] = x_ref[0] + _rmsnorm(m, g_ref[...])


def _attn_out(x3d, q, k, v, cp, w_out, g):
    b, s, _ = x3d.shape
    tile = lambda width: pl.BlockSpec((1, ATT_Q, width), lambda bi, qi: (bi, qi, 0))
    seq = lambda width: pl.BlockSpec((1, s, width), lambda bi, qi: (bi, 0, 0))
    return pl.pallas_call(
        _attn_out_kernel,
        name="attn_out",
        grid=(b, s // ATT_Q),
        in_specs=[
            tile(D_MODEL), tile(ATTN_DIM), seq(ATTN_DIM), seq(ATTN_DIM), tile(CONV_DIM + POOL_DIM),
            pl.BlockSpec((D_MODEL, D_MODEL), lambda bi, qi: (0, 0)),
            pl.BlockSpec((1, D_MODEL), lambda bi, qi: (0, 0)),
        ],
        out_specs=tile(D_MODEL),
        out_shape=jax.ShapeDtypeStruct(x3d.shape, F32),
        scratch_shapes=[
            pltpu.VMEM((ATTN_HEADS // 2, 2 * ATT_Q, LANES), BF16),
            pltpu.VMEM((ATT_Q, D_MODEL), BF16),
            pltpu.VMEM((ATTN_HEADS // 2, 2 * ATT_Q, LANES), F32),
            pltpu.VMEM((ATTN_HEADS // 2, 2 * ATT_Q, 1), F32),
        ],
        compiler_params=pltpu.CompilerParams(
            dimension_semantics=("arbitrary", "arbitrary"), vmem_limit_bytes=VMEM_LIMIT),
    )(x3d, q, k, v, cp, w_out, g)


def _ffn_weights(w_gate, w_up, w_down):
    cols = lambda w: w.astype(BF16).reshape(D_MODEL, N_FF_CHUNKS, FF_CHUNK).transpose(1, 0, 2)
    return cols(w_gate), cols(w_up), w_down.astype(BF16).reshape(N_FF_CHUNKS, FF_CHUNK, D_MODEL)


def _block_diag(pool_w):
    out = jnp.zeros((POOL_DIM, POOL_DIM), pool_w.dtype)
    for gi in range(len(POOL_WINDOWS)):
        sl = slice(gi * POOL_GD, (gi + 1) * POOL_GD)
        out = out.at[sl, sl].set(pool_w[gi])
    return out


def kernel(x, norm_g, ffn_w_gate, ffn_w_up, ffn_w_down, w_in, conv_w, pool_w, pool_scale, w_out):
    b, s, d = x.shape
    assert d == D_MODEL and s % PROJ_ROWS == 0 and s % ATT_Q == 0 and (b * s) % FFN_ROWS == 0
    depth = norm_g.shape[0]
    x2d = x.reshape(b * s, d)
    for l in range(depth):
        g = norm_g[l].reshape(6, 1, D_MODEL)
        x2d = _ffn(x2d, g[0], g[1], *_ffn_weights(ffn_w_gate[l, 0], ffn_w_up[l, 0], ffn_w_down[l, 0]))
        q, k, v, cp = _in_proj(x2d, s, g[2], w_in[l].astype(BF16), conv_w[l],
                               _block_diag(pool_w[l]).astype(BF16), pool_scale[l].reshape(1, POOL_DIM))
        r3 = lambda t: t.reshape(b, s, t.shape[-1])
        x2d = _attn_out(r3(x2d), r3(q), r3(k), r3(v), r3(cp), w_out[l].astype(BF16), g[3]).reshape(b * s, d)
        x2d = _ffn(x2d, g[4], g[5], *_ffn_weights(ffn_w_gate[l, 1], ffn_w_up[l, 1], ffn_w_down[l, 1]))
    return x2d.reshape(b, s, d)
```

```python
import functools
import math

import jax
import jax.numpy as jnp
from jax import lax
from jax.experimental import pallas as pl
from jax.experimental.pallas import tpu as pltpu

F32 = jnp.float32
BF16 = jnp.bfloat16

D_MODEL = 1024
D_FF = 2816
ATTN_HEADS = 8
ATTN_HEAD_DIM = 64
ATTN_DIM = ATTN_HEADS * ATTN_HEAD_DIM
CONV_DIM = 256
CONV_WIDTH = 3
POOL_WINDOWS = (2, 4, 8, 16)
POOL_DIM = 256
POOL_GD = POOL_DIM // len(POOL_WINDOWS)
IN_PROJ_WIDTH = 3 * ATTN_DIM + 3 * CONV_DIM + POOL_DIM
RMS_EPS = 1e-6

LANES = 128
FF_CHUNK = 256
N_FF_CHUNKS = D_FF // FF_CHUNK
FFN_ROWS = 512
PROJ_ROWS = 512
HALO = 32
ATT_Q = 256
ATT_K = 256
UNDERFLOW_LOG = -105.0
VMEM_LIMIT = 56 * 1024 * 1024


def _rmsnorm(x, g):
    ms = jnp.mean(x * x, axis=-1, keepdims=True)
    return x * lax.rsqrt(ms + RMS_EPS) * g


def _ffn_kernel(x_ref, gpre_ref, gpost_ref, wg_ref, wu_ref, wd_ref, o_ref):
    x = x_ref[...]
    h = _rmsnorm(x, gpre_ref[...]).astype(BF16)
    acc = jnp.zeros(x.shape, F32)
    for c in range(N_FF_CHUNKS):
        g = jnp.dot(h, wg_ref[c], preferred_element_type=F32)
        u = jnp.dot(h, wu_ref[c], preferred_element_type=F32)
        a = g * jax.nn.sigmoid(g) * u
        acc = acc + jnp.dot(a.astype(BF16), wd_ref[c], preferred_element_type=F32)
    o_ref[...] = x + 0.5 * _rmsnorm(acc, gpost_ref[...])


def _ffn(x2d, g_pre, g_post, wg, wu, wd):
    n = x2d.shape[0]
    row_spec = pl.BlockSpec((FFN_ROWS, D_MODEL), lambda i: (i, 0))
    gain_spec = pl.BlockSpec((1, D_MODEL), lambda i: (0, 0))
    return pl.pallas_call(
        _ffn_kernel,
        name="ffn",
        grid=(n // FFN_ROWS,),
        in_specs=[
            row_spec, gain_spec, gain_spec,
            pl.BlockSpec((N_FF_CHUNKS, D_MODEL, FF_CHUNK), lambda i: (0, 0, 0)),
            pl.BlockSpec((N_FF_CHUNKS, D_MODEL, FF_CHUNK), lambda i: (0, 0, 0)),
            pl.BlockSpec((N_FF_CHUNKS, FF_CHUNK, D_MODEL), lambda i: (0, 0, 0)),
        ],
        out_specs=row_spec,
        out_shape=jax.ShapeDtypeStruct(x2d.shape, F32),
        compiler_params=pltpu.CompilerParams(
            dimension_semantics=("arbitrary",), vmem_limit_bytes=VMEM_LIMIT),
    )(x2d, g_pre, g_post, wg, wu, wd)


def _in_proj_kernel(tiles_per_seq, x_ref, g_ref, w_ref, convw_ref, poolw_ref, pscale_ref,
                    q_ref, k_ref, v_ref, cp_ref, ubuf, pbuf, s2buf, s4buf, s8buf):
    rows = PROJ_ROWS
    tile_in_seq = pl.program_id(0) % tiles_per_seq

    @pl.when(tile_in_seq == 0)
    def _():
        ubuf[0:HALO, :] = jnp.zeros((HALO, CONV_DIM), F32)
        pbuf[0:HALO, :] = jnp.zeros((HALO, POOL_DIM), F32)

    @pl.when(tile_in_seq != 0)
    def _():
        ubuf[0:HALO, :] = ubuf[rows:rows + HALO, :]
        pbuf[0:HALO, :] = pbuf[rows:rows + HALO, :]

    h = _rmsnorm(x_ref[...], g_ref[...]).astype(BF16)
    proj = jnp.dot(h, w_ref[...], preferred_element_type=F32)
    a = ATTN_DIM
    q_ref[...] = (proj[:, 0:a] * (1.0 / math.sqrt(ATTN_HEAD_DIM))).astype(BF16)
    k_ref[...] = proj[:, a:2 * a].astype(BF16)
    v_ref[...] = proj[:, 2 * a:3 * a].astype(BF16)
    c0 = 3 * a
    gate_b = proj[:, c0:c0 + CONV_DIM]
    gate_c = proj[:, c0 + CONV_DIM:c0 + 2 * CONV_DIM]
    conv_h = proj[:, c0 + 2 * CONV_DIM:c0 + 3 * CONV_DIM]
    p = proj[:, c0 + 3 * CONV_DIM:]

    ubuf[HALO:HALO + rows, :] = gate_c * conv_h
    cw = convw_ref[...]
    y = (cw[0:1, :] * ubuf[HALO - 2:HALO - 2 + rows, :]
         + cw[1:2, :] * ubuf[HALO - 1:HALO - 1 + rows, :]
         + cw[2:3, :] * ubuf[HALO:HALO + rows, :])
    cp_ref[:, 0:CONV_DIM] = (gate_b * y).astype(BF16)

    pbuf[HALO:HALO + rows, :] = p
    end = HALO + rows
    s2buf[8:end, :] = pbuf[8:end, :] + pbuf[7:end - 1, :]
    s4buf[16:end, :] = s2buf[16:end, :] + s2buf[14:end - 2, :]
    s8buf[24:end, :] = s4buf[24:end, :] + s4buf[20:end - 4, :]
    s16 = s8buf[HALO:end, :] + s8buf[HALO - 8:end - 8, :]
    group = lax.broadcasted_iota(jnp.int32, (rows, POOL_DIM), 1) // POOL_GD
    win_sum = jnp.where(group == 0, s2buf[HALO:end, :],
                        jnp.where(group == 1, s4buf[HALO:end, :],
                                  jnp.where(group == 2, s8buf[HALO:end, :], s16)))
    window = jnp.where(group == 0, POOL_WINDOWS[0],
                       jnp.where(group == 1, POOL_WINDOWS[1],
                                 jnp.where(group == 2, POOL_WINDOWS[2], POOL_WINDOWS[3])))
    pos = tile_in_seq * rows + lax.broadcasted_iota(jnp.int32, (rows, POOL_DIM), 0)
    count = jnp.minimum(pos + 1, window).astype(F32)
    d = win_sum / count - p
    pooled = jnp.dot(d.astype(BF16), poolw_ref[...], preferred_element_type=F32)
    cp_ref[:, CONV_DIM:] = (pooled * pscale_ref[...]).astype(BF16)


def _in_proj(x2d, seq, g, w_in, conv_w, pool_w_bd, pool_scale):
    n = x2d.shape[0]
    tiles_per_seq = seq // PROJ_ROWS
    row = lambda width: pl.BlockSpec((PROJ_ROWS, width), lambda i: (i, 0))
    full = lambda shape: pl.BlockSpec(shape, lambda i: (0,) * len(shape))
    att = jax.ShapeDtypeStruct((n, ATTN_DIM), BF16)
    buf = lambda width: pltpu.VMEM((HALO + PROJ_ROWS, width), F32)
    return pl.pallas_call(
        functools.partial(_in_proj_kernel, tiles_per_seq),
        name="in_proj",
        grid=(n // PROJ_ROWS,),
        in_specs=[
            row(D_MODEL), full((1, D_MODEL)), full((D_MODEL, IN_PROJ_WIDTH)),
            full((CONV_WIDTH, CONV_DIM)), full((POOL_DIM, POOL_DIM)), full((1, POOL_DIM)),
        ],
        out_specs=[row(ATTN_DIM), row(ATTN_DIM), row(ATTN_DIM), row(CONV_DIM + POOL_DIM)],
        out_shape=[att, att, att, jax.ShapeDtypeStruct((n, CONV_DIM + POOL_DIM), BF16)],
        scratch_shapes=[buf(CONV_DIM), buf(POOL_DIM), buf(POOL_DIM), buf(POOL_DIM), buf(POOL_DIM)],
        compiler_params=pltpu.CompilerParams(
            dimension_semantics=("arbitrary",), vmem_limit_bytes=VMEM_LIMIT),
    )(x2d, g, w_in, conv_w, pool_w_bd, pool_scale)


def _attn_block(qm, k2, v2, carry, neg_tri, mask):
    z = lax.dot_general(qm, k2, (((1,), (1,)), ((), ())), preferred_element_type=F32)
    neg_abs = lax.bitcast_convert_type(
        lax.bitcast_convert_type(z, jnp.uint32) | jnp.uint32(0x80000000), F32)
    soft = jnp.log(1.0 + jnp.exp(neg_abs))
    drop = jnp.maximum(z, 0.0) + soft
    log_beta = z - drop
    if mask is not None:
        drop = jnp.where(mask, drop, 0.0)
    between = jnp.dot(drop.astype(BF16), neg_tri, preferred_element_type=F32)
    a = jnp.exp(log_beta + between + carry)
    if mask is not None:
        a = jnp.where(mask, a, 0.0)
    out = jnp.dot(a.astype(BF16), v2, preferred_element_type=F32)
    new_carry = carry + between[:, 0:1] - drop[:, 0:1]
    return out, new_carry


def _attn_out_kernel(x_ref, q_ref, k_ref, v_ref, cp_ref, wout_ref, g_ref, o_ref,
                     qm_ref, mix_ref, acc_ref, carry_ref):
    qi = pl.program_id(1)
    n_pairs = ATTN_HEADS // 2
    row = lax.broadcasted_iota(jnp.int32, (ATT_Q, ATT_K), 0)
    col = lax.broadcasted_iota(jnp.int32, (ATT_Q, ATT_K), 1)
    causal = jnp.concatenate([col < row] * 2, axis=0)
    neg_tri = jnp.where(row > col, -1.0, 0.0).astype(BF16)
    lane = lax.broadcasted_iota(jnp.int32, (ATT_Q, LANES), 1)
    first_head = lane < ATTN_HEAD_DIM
    pair_lanes = [slice(hp * LANES, (hp + 1) * LANES) for hp in range(n_pairs)]

    for hp in range(n_pairs):
        q2 = q_ref[0, :, pair_lanes[hp]]
        qm_ref[hp, 0:ATT_Q, :] = jnp.where(first_head, q2, jnp.zeros_like(q2))
        qm_ref[hp, ATT_Q:, :] = jnp.where(first_head, jnp.zeros_like(q2), q2)

    def block(start, mask, is_first):
        for hp in range(n_pairs):
            k2 = k_ref[0, pl.ds(start, ATT_K), pair_lanes[hp]]
            v2 = v_ref[0, pl.ds(start, ATT_K), pair_lanes[hp]]
            carry = jnp.zeros((2 * ATT_Q, 1), F32) if is_first else carry_ref[hp]
            out, carry = _attn_block(qm_ref[hp], k2, v2, carry, neg_tri, mask)
            acc_ref[hp] = out if is_first else acc_ref[hp] + out
            carry_ref[hp] = carry

    block(pl.multiple_of(qi * ATT_K, ATT_K), causal, True)

    def any_weight_left():
        return jnp.max(carry_ref[...]) > UNDERFLOW_LOG

    def body(state):
        step, _ = state
        block(pl.multiple_of((qi - 1 - step) * ATT_K, ATT_K), None, False)
        return step + 1, any_weight_left()

    lax.while_loop(lambda state: jnp.logical_and(state[0] < qi, state[1]), body,
                   (jnp.int32(0), any_weight_left()))

    for hp in range(n_pairs):
        mix_ref[:, pair_lanes[hp]] = jnp.where(
            first_head, acc_ref[hp, 0:ATT_Q, :], acc_ref[hp, ATT_Q:, :]).astype(BF16)
    mix_ref[:, ATTN_DIM:] = cp_ref[0]
    m = jnp.dot(mix_ref[...], wout_ref[...], preferred_element_type=F32)
    o_ref[0] = x_ref[0] + _rmsnorm(m, g_ref[...])


def _attn_out(x3d, q, k, v, cp, w_out, g):
    b, s, _ = x3d.shape
    tile = lambda width: pl.BlockSpec((1, ATT_Q, width), lambda bi, qi: (bi, qi, 0))
    seq = lambda width: pl.BlockSpec((1, s, width), lambda bi, qi: (bi, 0, 0))
    return pl.pallas_call(
        _attn_out_kernel,
        name="attn_out",
        grid=(b, s // ATT_Q),
        in_specs=[
            tile(D_MODEL), tile(ATTN_DIM), seq(ATTN_DIM), seq(ATTN_DIM), tile(CONV_DIM + POOL_DIM),
            pl.BlockSpec((D_MODEL, D_MODEL), lambda bi, qi: (0, 0)),
            pl.BlockSpec((1, D_MODEL), lambda bi, qi: (0, 0)),
        ],
        out_specs=tile(D_MODEL),
        out_shape=jax.ShapeDtypeStruct(x3d.shape, F32),
        scratch_shapes=[
            pltpu.VMEM((ATTN_HEADS // 2, 2 * ATT_Q, LANES), BF16),
            pltpu.VMEM((ATT_Q, D_MODEL), BF16),
            pltpu.VMEM((ATTN_HEADS // 2, 2 * ATT_Q, LANES), F32),
            pltpu.VMEM((ATTN_HEADS // 2, 2 * ATT_Q, 1), F32),
        ],
        compiler_params=pltpu.CompilerParams(
            dimension_semantics=("arbitrary", "arbitrary"), vmem_limit_bytes=VMEM_LIMIT),
    )(x3d, q, k, v, cp, w_out, g)


def _ffn_weights(w_gate, w_up, w_down):
    cols = lambda w: w.astype(BF16).reshape(D_MODEL, N_FF_CHUNKS, FF_CHUNK).transpose(1, 0, 2)
    return cols(w_gate), cols(w_up), w_down.astype(BF16).reshape(N_FF_CHUNKS, FF_CHUNK, D_MODEL)


def _block_diag(pool_w):
    out = jnp.zeros((POOL_DIM, POOL_DIM), pool_w.dtype)
    for gi in range(len(POOL_WINDOWS)):
        sl = slice(gi * POOL_GD, (gi + 1) * POOL_GD)
        out = out.at[sl, sl].set(pool_w[gi])
    return out


def kernel(x, norm_g, ffn_w_gate, ffn_w_up, ffn_w_down, w_in, conv_w, pool_w, pool_scale, w_out):
    b, s, d = x.shape
    assert d == D_MODEL and s % PROJ_ROWS == 0 and s % ATT_Q == 0 and (b * s) % FFN_ROWS == 0
    depth = norm_g.shape[0]
    x2d = x.reshape(b * s, d)
    for l in range(depth):
        g = norm_g[l].reshape(6, 1, D_MODEL)
        x2d = _ffn(x2d, g[0], g[1], *_ffn_weights(ffn_w_gate[l, 0], ffn_w_up[l, 0], ffn_w_down[l, 0]))
        q, k, v, cp = _in_proj(x2d, s, g[2], w_in[l].astype(BF16), conv_w[l],
                               _block_diag(pool_w[l]).astype(BF16), pool_scale[l].reshape(1, POOL_DIM))
        r3 = lambda t: t.reshape(b, s, t.shape[-1])
        x2d = _attn_out(r3(x2d), r3(q), r3(k), r3(v), r3(cp), w_out[l].astype(BF16), g[3]).reshape(b * s, d)
        x2d = _ffn(x2d, g[4], g[5], *_ffn_weights(ffn_w_gate[l, 1], ffn_w_up[l, 1], ffn_w_down[l, 1]))
    return x2d.reshape(b, s, d)
```

```python
import functools
import math

import jax
import jax.numpy as jnp
from jax import lax
from jax.experimental import pallas as pl
from jax.experimental.pallas import tpu as pltpu

F32 = jnp.float32
BF16 = jnp.bfloat16

D_MODEL = 1024
D_FF = 2816
ATTN_HEADS = 8
ATTN_HEAD_DIM = 64
ATTN_DIM = ATTN_HEADS * ATTN_HEAD_DIM
CONV_DIM = 256
CONV_WIDTH = 3
POOL_WINDOWS = (2, 4, 8, 16)
POOL_DIM = 256
POOL_GD = POOL_DIM // len(POOL_WINDOWS)
IN_PROJ_WIDTH = 3 * ATTN_DIM + 3 * CONV_DIM + POOL_DIM
RMS_EPS = 1e-6

LANES = 128
FF_CHUNK = 256
N_FF_CHUNKS = D_FF // FF_CHUNK
FFN_ROWS = 1024
FFN_GROUP_ROWS = 512
PROJ_ROWS = 1024
PROJ_GROUP_ROWS = 512
HALO = 32
ATT_Q = 256
ATT_K = 256
UNDERFLOW_LOG = -105.0
VMEM_LIMIT = 56 * 1024 * 1024


def _rmsnorm(x, g):
    ms = jnp.mean(x * x, axis=-1, keepdims=True)
    return x * lax.rsqrt(ms + RMS_EPS) * g


def _ffn_kernel(x_ref, gpre_ref, gpost_ref, wg_ref, wu_ref, wd_ref, o_ref):
    groups = [slice(r, r + FFN_GROUP_ROWS) for r in range(0, FFN_ROWS, FFN_GROUP_ROWS)]
    hs = [_rmsnorm(x_ref[rows, :], gpre_ref[...]).astype(BF16) for rows in groups]
    accs = [jnp.zeros((FFN_GROUP_ROWS, D_MODEL), F32) for _ in groups]
    for c in range(N_FF_CHUNKS):
        cols = slice(c * FF_CHUNK, (c + 1) * FF_CHUNK)
        for i, h in enumerate(hs):
            g = jnp.dot(h, wg_ref[:, cols], preferred_element_type=F32)
            u = jnp.dot(h, wu_ref[:, cols], preferred_element_type=F32)
            a = g * jax.nn.sigmoid(g) * u
            accs[i] = accs[i] + jnp.dot(a.astype(BF16), wd_ref[cols, :], preferred_element_type=F32)
    for rows, acc in zip(groups, accs):
        o_ref[rows, :] = x_ref[rows, :] + 0.5 * _rmsnorm(acc, gpost_ref[...])


def _ffn(x2d, g_pre, g_post, wg, wu, wd):
    n = x2d.shape[0]
    row_spec = pl.BlockSpec((FFN_ROWS, D_MODEL), lambda i: (i, 0))
    gain_spec = pl.BlockSpec((1, D_MODEL), lambda i: (0, 0))
    resident = lambda shape: pl.BlockSpec(shape, lambda i: (0, 0), pipeline_mode=pl.Buffered(1))
    return pl.pallas_call(
        _ffn_kernel,
        name="ffn",
        grid=(n // FFN_ROWS,),
        in_specs=[
            row_spec, gain_spec, gain_spec,
            resident((D_MODEL, D_FF)), resident((D_MODEL, D_FF)), resident((D_FF, D_MODEL)),
        ],
        out_specs=row_spec,
        out_shape=jax.ShapeDtypeStruct(x2d.shape, F32),
        compiler_params=pltpu.CompilerParams(
            dimension_semantics=("arbitrary",), vmem_limit_bytes=VMEM_LIMIT),
    )(x2d, g_pre, g_post, wg, wu, wd)


def _in_proj_kernel(tiles_per_seq, x_ref, g_ref, w_ref, convw_ref, poolw_ref, pscale_ref,
                    q_ref, k_ref, v_ref, cp_ref, ubuf, pbuf, s2buf, s4buf, s8buf):
    tile_in_seq = pl.program_id(0) % tiles_per_seq

    @pl.when(tile_in_seq == 0)
    def _():
        ubuf[0:HALO, :] = jnp.zeros((HALO, CONV_DIM), F32)
        pbuf[0:HALO, :] = jnp.zeros((HALO, POOL_DIM), F32)

    @pl.when(tile_in_seq != 0)
    def _():
        ubuf[0:HALO, :] = ubuf[PROJ_ROWS:PROJ_ROWS + HALO, :]
        pbuf[0:HALO, :] = pbuf[PROJ_ROWS:PROJ_ROWS + HALO, :]

    starts = list(range(0, PROJ_ROWS, PROJ_GROUP_ROWS))
    mix_cols = slice(3 * ATTN_DIM, IN_PROJ_WIDTH)
    h = _rmsnorm(x_ref[0:PROJ_GROUP_ROWS, :], g_ref[...]).astype(BF16)
    for gi, r0 in enumerate(starts):
        out_rows = slice(r0, r0 + PROJ_GROUP_ROWS)
        mix_in = jnp.dot(h, w_ref[:, mix_cols], preferred_element_type=F32)
        q = jnp.dot(h, w_ref[:, 0:ATTN_DIM], preferred_element_type=F32)
        q_ref[out_rows, :] = (q * (1.0 / math.sqrt(ATTN_HEAD_DIM))).astype(BF16)
        _mix_rows(r0, tile_in_seq * PROJ_ROWS + r0, mix_in, convw_ref, poolw_ref, pscale_ref,
                  cp_ref, ubuf, pbuf, s2buf, s4buf, s8buf)
        k_ref[out_rows, :] = jnp.dot(h, w_ref[:, ATTN_DIM:2 * ATTN_DIM],
                                     preferred_element_type=F32).astype(BF16)
        h_next = None
        if gi + 1 < len(starts):
            nxt = starts[gi + 1]
            h_next = _rmsnorm(x_ref[nxt:nxt + PROJ_GROUP_ROWS, :], g_ref[...]).astype(BF16)
        v_ref[out_rows, :] = jnp.dot(h, w_ref[:, 2 * ATTN_DIM:3 * ATTN_DIM],
                                     preferred_element_type=F32).astype(BF16)
        h = h_next


def _mix_rows(r0, seq_pos, mix_in, convw_ref, poolw_ref, pscale_ref,
              cp_ref, ubuf, pbuf, s2buf, s4buf, s8buf):
    rows = PROJ_GROUP_ROWS
    out_rows = slice(r0, r0 + rows)
    gate_b = mix_in[:, 0:CONV_DIM]
    gate_c = mix_in[:, CONV_DIM:2 * CONV_DIM]
    conv_h = mix_in[:, 2 * CONV_DIM:3 * CONV_DIM]
    p = mix_in[:, 3 * CONV_DIM:]

    tok = r0 + HALO
    end = tok + rows
    ubuf[tok:end, :] = gate_c * conv_h
    cw = convw_ref[...]
    y = (cw[0:1, :] * ubuf[tok - 2:end - 2, :]
         + cw[1:2, :] * ubuf[tok - 1:end - 1, :]
         + cw[2:3, :] * ubuf[tok:end, :])
    cp_ref[out_rows, 0:CONV_DIM] = (gate_b * y).astype(BF16)

    pbuf[tok:end, :] = p
    s2buf[r0 + 8:end, :] = pbuf[r0 + 8:end, :] + pbuf[r0 + 7:end - 1, :]
    s4buf[r0 + 16:end, :] = s2buf[r0 + 16:end, :] + s2buf[r0 + 14:end - 2, :]
    s8buf[r0 + 24:end, :] = s4buf[r0 + 24:end, :] + s4buf[r0 + 20:end - 4, :]
    s16 = s8buf[tok:end, :] + s8buf[tok - 8:end - 8, :]
    group = lax.broadcasted_iota(jnp.int32, (rows, POOL_DIM), 1) // POOL_GD
    win_sum = jnp.where(group == 0, s2buf[tok:end, :],
                        jnp.where(group == 1, s4buf[tok:end, :],
                                  jnp.where(group == 2, s8buf[tok:end, :], s16)))
    window = jnp.where(group == 0, POOL_WINDOWS[0],
                       jnp.where(group == 1, POOL_WINDOWS[1],
                                 jnp.where(group == 2, POOL_WINDOWS[2], POOL_WINDOWS[3])))
    pos = seq_pos + lax.broadcasted_iota(jnp.int32, (rows, POOL_DIM), 0)
    count = jnp.minimum(pos + 1, window).astype(F32)
    d = win_sum / count - p
    pooled = jnp.dot(d.astype(BF16), poolw_ref[...], preferred_element_type=F32)
    cp_ref[out_rows, CONV_DIM:] = (pooled * pscale_ref[...]).astype(BF16)


def _in_proj(x2d, seq, g, w_in, conv_w, pool_w_bd, pool_scale):
    n = x2d.shape[0]
    tiles_per_seq = seq // PROJ_ROWS
    row = lambda width: pl.BlockSpec((PROJ_ROWS, width), lambda i: (i, 0))
    full = lambda shape: pl.BlockSpec(shape, lambda i: (0,) * len(shape))
    att = jax.ShapeDtypeStruct((n, ATTN_DIM), BF16)
    buf = lambda width: pltpu.VMEM((HALO + PROJ_ROWS, width), F32)
    return pl.pallas_call(
        functools.partial(_in_proj_kernel, tiles_per_seq),
        name="in_proj",
        grid=(n // PROJ_ROWS,),
        in_specs=[
            row(D_MODEL), full((1, D_MODEL)), full((D_MODEL, IN_PROJ_WIDTH)),
            full((CONV_WIDTH, CONV_DIM)), full((POOL_DIM, POOL_DIM)), full((1, POOL_DIM)),
        ],
        out_specs=[row(ATTN_DIM), row(ATTN_DIM), row(ATTN_DIM), row(CONV_DIM + POOL_DIM)],
        out_shape=[att, att, att, jax.ShapeDtypeStruct((n, CONV_DIM + POOL_DIM), BF16)],
        scratch_shapes=[buf(CONV_DIM), buf(POOL_DIM), buf(POOL_DIM), buf(POOL_DIM), buf(POOL_DIM)],
        compiler_params=pltpu.CompilerParams(
            dimension_semantics=("arbitrary",), vmem_limit_bytes=VMEM_LIMIT),
    )(x2d, g, w_in, conv_w, pool_w_bd, pool_scale)


def _attn_block(qm, k2, v2, carry, neg_tri, mask):
    z = lax.dot_general(qm, k2, (((1,), (1,)), ((), ())), preferred_element_type=F32)
    neg_abs = lax.bitcast_convert_type(
        lax.bitcast_convert_type(z, jnp.uint32) | jnp.uint32(0x80000000), F32)
    soft = jnp.log(1.0 + jnp.exp(neg_abs))
    drop = jnp.maximum(z, 0.0) + soft
    log_beta = z - drop
    if mask is not None:
        drop = jnp.where(mask, drop, 0.0)
    between = jnp.dot(drop.astype(BF16), neg_tri, preferred_element_type=F32)
    a = jnp.exp(log_beta + between + carry)
    if mask is not None:
        a = jnp.where(mask, a, 0.0)
    out = jnp.dot(a.astype(BF16), v2, preferred_element_type=F32)
    new_carry = carry + between[:, 0:1] - drop[:, 0:1]
    return out, new_carry


def _attn_out_kernel(x_ref, q_ref, k_ref, v_ref, cp_ref, wout_ref, g_ref, o_ref,
                     qm_ref, mix_ref, acc_ref, carry_ref):
    qi = pl.program_id(1)
    n_pairs = ATTN_HEADS // 2
    row = lax.broadcasted_iota(jnp.int32, (ATT_Q, ATT_K), 0)
    col = lax.broadcasted_iota(jnp.int32, (ATT_Q, ATT_K), 1)
    causal = jnp.concatenate([col < row] * 2, axis=0)
    neg_tri = jnp.where(row > col, -1.0, 0.0).astype(BF16)
    lane = lax.broadcasted_iota(jnp.int32, (ATT_Q, LANES), 1)
    first_head = lane < ATTN_HEAD_DIM
    pair_lanes = [slice(hp * LANES, (hp + 1) * LANES) for hp in range(n_pairs)]

    for hp in range(n_pairs):
        q2 = q_ref[0, :, pair_lanes[hp]]
        qm_ref[hp, 0:ATT_Q, :] = jnp.where(first_head, q2, jnp.zeros_like(q2))
        qm_ref[hp, ATT_Q:, :] = jnp.where(first_head, jnp.zeros_like(q2), q2)

    def block(start, mask, is_first):
        for hp in range(n_pairs):
            k2 = k_ref[0, pl.ds(start, ATT_K), pair_lanes[hp]]
            v2 = v_ref[0, pl.ds(start, ATT_K), pair_lanes[hp]]
            carry = jnp.zeros((2 * ATT_Q, 1), F32) if is_first else carry_ref[hp]
            out, carry = _attn_block(qm_ref[hp], k2, v2, carry, neg_tri, mask)
            acc_ref[hp] = out if is_first else acc_ref[hp] + out
            carry_ref[hp] = carry

    block(pl.multiple_of(qi * ATT_K, ATT_K), causal, True)

    def any_weight_left():
        return jnp.max(carry_ref[...]) > UNDERFLOW_LOG

    def body(state):
        step, _ = state
        block(pl.multiple_of((qi - 1 - step) * ATT_K, ATT_K), None, False)
        return step + 1, any_weight_left()

    lax.while_loop(lambda state: jnp.logical_and(state[0] < qi, state[1]), body,
                   (jnp.int32(0), any_weight_left()))

    for hp in range(n_pairs):
        mix_ref[:, pair_lanes[hp]] = jnp.where(
            first_head, acc_ref[hp, 0:ATT_Q, :], acc_ref[hp, ATT_Q:, :]).astype(BF16)
    mix_ref[:, ATTN_DIM:] = cp_ref[0]
    m = jnp.dot(mix_ref[...], wout_ref[...], preferred_element_type=F32)
    o_ref[0] = x_ref[0] + _rmsnorm(m, g_ref[...])


def _attn_out(x3d, q, k, v, cp, w_out, g):
    b, s, _ = x3d.shape
    tile = lambda width: pl.BlockSpec((1, ATT_Q, width), lambda bi, qi: (bi, qi, 0))
    seq = lambda width: pl.BlockSpec((1, s, width), lambda bi, qi: (bi, 0, 0))
    return pl.pallas_call(
        _attn_out_kernel,
        name="attn_out",
        grid=(b, s // ATT_Q),
        in_specs=[
            tile(D_MODEL), tile(ATTN_DIM), seq(ATTN_DIM), seq(ATTN_DIM), tile(CONV_DIM + POOL_DIM),
            pl.BlockSpec((D_MODEL, D_MODEL), lambda bi, qi: (0, 0)),
            pl.BlockSpec((1, D_MODEL), lambda bi, qi: (0, 0)),
        ],
        out_specs=tile(D_MODEL),
        out_shape=jax.ShapeDtypeStruct(x3d.shape, F32),
        scratch_shapes=[
            pltpu.VMEM((ATTN_HEADS // 2, 2 * ATT_Q, LANES), BF16),
            pltpu.VMEM((ATT_Q, D_MODEL), BF16),
            pltpu.VMEM((ATTN_HEADS // 2, 2 * ATT_Q, LANES), F32),
            pltpu.VMEM((ATTN_HEADS // 2, 2 * ATT_Q, 1), F32),
        ],
        compiler_params=pltpu.CompilerParams(
            dimension_semantics=("arbitrary", "arbitrary"), vmem_limit_bytes=VMEM_LIMIT),
    )(x3d, q, k, v, cp, w_out, g)


def _ffn_weights(w_gate, w_up, w_down):
    return w_gate.astype(BF16), w_up.astype(BF16), w_down.astype(BF16)


def _block_diag(pool_w):
    out = jnp.zeros((POOL_DIM, POOL_DIM), pool_w.dtype)
    for gi in range(len(POOL_WINDOWS)):
        sl = slice(gi * POOL_GD, (gi + 1) * POOL_GD)
        out = out.at[sl, sl].set(pool_w[gi])
    return out


def kernel(x, norm_g, ffn_w_gate, ffn_w_up, ffn_w_down, w_in, conv_w, pool_w, pool_scale, w_out):
    b, s, d = x.shape
    assert d == D_MODEL and s % PROJ_ROWS == 0 and s % ATT_Q == 0 and (b * s) % FFN_ROWS == 0
    depth = norm_g.shape[0]
    x2d = x.reshape(b * s, d)
    for l in range(depth):
        g = norm_g[l].reshape(6, 1, D_MODEL)
        x2d = _ffn(x2d, g[0], g[1], *_ffn_weights(ffn_w_gate[l, 0], ffn_w_up[l, 0], ffn_w_down[l, 0]))
        q, k, v, cp = _in_proj(x2d, s, g[2], w_in[l].astype(BF16), conv_w[l],
                               _block_diag(pool_w[l]).astype(BF16), pool_scale[l].reshape(1, POOL_DIM))
        r3 = lambda t: t.reshape(b, s, t.shape[-1])
        x2d = _attn_out(r3(x2d), r3(q), r3(k), r3(v), r3(cp), w_out[l].astype(BF16), g[3]).reshape(b * s, d)
        x2d = _ffn(x2d, g[4], g[5], *_ffn_weights(ffn_w_gate[l, 1], ffn_w_up[l, 1], ffn_w_down[l, 1]))
    return x2d.reshape(b, s, d)
```

```python
import functools
import math

import jax
import jax.numpy as jnp
from jax import lax
from jax.experimental import pallas as pl
from jax.experimental.pallas import tpu as pltpu

F32 = jnp.float32
BF16 = jnp.bfloat16

D_MODEL = 1024
D_FF = 2816
ATTN_HEADS = 8
ATTN_HEAD_DIM = 64
ATTN_DIM = ATTN_HEADS * ATTN_HEAD_DIM
CONV_DIM = 256
CONV_WIDTH = 3
POOL_WINDOWS = (2, 4, 8, 16)
POOL_DIM = 256
POOL_GD = POOL_DIM // len(POOL_WINDOWS)
IN_PROJ_WIDTH = 3 * ATTN_DIM + 3 * CONV_DIM + POOL_DIM
RMS_EPS = 1e-6

LANES = 128
FF_CHUNK = 256
N_FF_CHUNKS = D_FF // FF_CHUNK
FFN_ROWS = 512
FFN_GROUP_ROWS = 512
PROJ_ROWS = 1024
PROJ_GROUP_ROWS = 512
HALO = 32
ATT_Q = 256
ATT_K = 256
UNDERFLOW_LOG = -105.0
VMEM_LIMIT = 56 * 1024 * 1024


def _rmsnorm(x, g):
    ms = jnp.mean(x * x, axis=-1, keepdims=True)
    return x * lax.rsqrt(ms + RMS_EPS) * g


def _ffn_kernel(x_ref, gpre_ref, gpost_ref, wg_ref, wu_ref, wd_ref, o_ref):
    groups = [slice(r, r + FFN_GROUP_ROWS) for r in range(0, FFN_ROWS, FFN_GROUP_ROWS)]
    hs = [_rmsnorm(x_ref[rows, :], gpre_ref[...]) for rows in groups]
    accs = [jnp.zeros((FFN_GROUP_ROWS, D_MODEL), F32) for _ in groups]
    for c in range(N_FF_CHUNKS):
        cols = slice(c * FF_CHUNK, (c + 1) * FF_CHUNK)
        for i, h in enumerate(hs):
            g = jnp.dot(h, wg_ref[:, cols], preferred_element_type=F32)
            u = jnp.dot(h, wu_ref[:, cols], preferred_element_type=F32)
            a = g * jax.nn.sigmoid(g) * u
            accs[i] = accs[i] + jnp.dot(a, wd_ref[cols, :], preferred_element_type=F32)
    for rows, acc in zip(groups, accs):
        o_ref[rows, :] = x_ref[rows, :] + 0.5 * _rmsnorm(acc, gpost_ref[...])


def _ffn(x2d, norm_g, w_gate, w_up, w_down, layer, half):
    n = x2d.shape[0]
    row_spec = pl.BlockSpec((FFN_ROWS, D_MODEL), lambda i: (i, 0))
    gain = lambda which: pl.BlockSpec((None, None, 1, D_MODEL), lambda i: (layer, which, 0, 0))
    resident = lambda rows, cols: pl.BlockSpec(
        (None, None, rows, cols), lambda i: (layer, half, 0, 0), pipeline_mode=pl.Buffered(1))
    pre, post = (0, 1) if half == 0 else (4, 5)
    return pl.pallas_call(
        _ffn_kernel,
        name="ffn",
        grid=(n // FFN_ROWS,),
        in_specs=[
            row_spec, gain(pre), gain(post),
            resident(D_MODEL, D_FF), resident(D_MODEL, D_FF), resident(D_FF, D_MODEL),
        ],
        out_specs=row_spec,
        out_shape=jax.ShapeDtypeStruct(x2d.shape, F32),
        compiler_params=pltpu.CompilerParams(
            dimension_semantics=("arbitrary",), vmem_limit_bytes=VMEM_LIMIT),
    )(x2d, norm_g, norm_g, w_gate, w_up, w_down)


def _in_proj_kernel(tiles_per_seq, x_ref, g_ref, w_ref, convw_ref, poolw_ref, pscale_ref,
                    q_ref, k_ref, v_ref, cp_ref, ubuf, pbuf, s2buf, s4buf, s8buf):
    tile_in_seq = pl.program_id(0) % tiles_per_seq

    @pl.when(tile_in_seq == 0)
    def _():
        ubuf[0:HALO, :] = jnp.zeros((HALO, CONV_DIM), F32)
        pbuf[0:HALO, :] = jnp.zeros((HALO, POOL_DIM), F32)

    @pl.when(tile_in_seq != 0)
    def _():
        ubuf[0:HALO, :] = ubuf[PROJ_ROWS:PROJ_ROWS + HALO, :]
        pbuf[0:HALO, :] = pbuf[PROJ_ROWS:PROJ_ROWS + HALO, :]

    starts = list(range(0, PROJ_ROWS, PROJ_GROUP_ROWS))
    mix_cols = slice(3 * ATTN_DIM, IN_PROJ_WIDTH)
    h = _rmsnorm(x_ref[0:PROJ_GROUP_ROWS, :], g_ref[...])
    for gi, r0 in enumerate(starts):
        out_rows = slice(r0, r0 + PROJ_GROUP_ROWS)
        mix_in = jnp.dot(h, w_ref[:, mix_cols], preferred_element_type=F32)
        q = jnp.dot(h, w_ref[:, 0:ATTN_DIM], preferred_element_type=F32)
        q_ref[out_rows, :] = (q * (1.0 / math.sqrt(ATTN_HEAD_DIM))).astype(BF16)
        _mix_rows(r0, tile_in_seq * PROJ_ROWS + r0, mix_in, convw_ref, poolw_ref, pscale_ref,
                  cp_ref, ubuf, pbuf, s2buf, s4buf, s8buf)
        k_ref[out_rows, :] = jnp.dot(h, w_ref[:, ATTN_DIM:2 * ATTN_DIM],
                                     preferred_element_type=F32).astype(BF16)
        h_next = None
        if gi + 1 < len(starts):
            nxt = starts[gi + 1]
            h_next = _rmsnorm(x_ref[nxt:nxt + PROJ_GROUP_ROWS, :], g_ref[...])
        v_ref[out_rows, :] = jnp.dot(h, w_ref[:, 2 * ATTN_DIM:3 * ATTN_DIM],
                                     preferred_element_type=F32).astype(BF16)
        h = h_next


def _mix_rows(r0, seq_pos, mix_in, convw_ref, poolw_ref, pscale_ref,
              cp_ref, ubuf, pbuf, s2buf, s4buf, s8buf):
    rows = PROJ_GROUP_ROWS
    out_rows = slice(r0, r0 + rows)
    gate_b = mix_in[:, 0:CONV_DIM]
    gate_c = mix_in[:, CONV_DIM:2 * CONV_DIM]
    conv_h = mix_in[:, 2 * CONV_DIM:3 * CONV_DIM]
    p = mix_in[:, 3 * CONV_DIM:]

    tok = r0 + HALO
    end = tok + rows
    ubuf[tok:end, :] = gate_c * conv_h
    cw = convw_ref[...]
    y = (cw[0:1, :] * ubuf[tok - 2:end - 2, :]
         + cw[1:2, :] * ubuf[tok - 1:end - 1, :]
         + cw[2:3, :] * ubuf[tok:end, :])
    cp_ref[out_rows, 0:CONV_DIM] = (gate_b * y).astype(BF16)

    pbuf[tok:end, :] = p
    s2buf[r0 + 8:end, :] = pbuf[r0 + 8:end, :] + pbuf[r0 + 7:end - 1, :]
    s4buf[r0 + 16:end, :] = s2buf[r0 + 16:end, :] + s2buf[r0 + 14:end - 2, :]
    s8buf[r0 + 24:end, :] = s4buf[r0 + 24:end, :] + s4buf[r0 + 20:end - 4, :]
    s16 = s8buf[tok:end, :] + s8buf[tok - 8:end - 8, :]
    group = lax.broadcasted_iota(jnp.int32, (rows, POOL_DIM), 1) // POOL_GD
    win_sum = jnp.where(group == 0, s2buf[tok:end, :],
                        jnp.where(group == 1, s4buf[tok:end, :],
                                  jnp.where(group == 2, s8buf[tok:end, :], s16)))
    window = jnp.where(group == 0, POOL_WINDOWS[0],
                       jnp.where(group == 1, POOL_WINDOWS[1],
                                 jnp.where(group == 2, POOL_WINDOWS[2], POOL_WINDOWS[3])))
    pos = seq_pos + lax.broadcasted_iota(jnp.int32, (rows, POOL_DIM), 0)
    count = jnp.minimum(pos + 1, window).astype(F32)
    d = win_sum / count - p
    pooled = jnp.dot(d, poolw_ref[...], preferred_element_type=F32)
    cp_ref[out_rows, CONV_DIM:] = (pooled * pscale_ref[...]).astype(BF16)


def _in_proj(x2d, seq, norm_g, w_in, conv_w, pool_w_bd, pool_scale, layer):
    n = x2d.shape[0]
    tiles_per_seq = seq // PROJ_ROWS
    row = lambda width: pl.BlockSpec((PROJ_ROWS, width), lambda i: (i, 0))
    of_layer = lambda rows, cols, **kw: pl.BlockSpec((None, rows, cols), lambda i: (layer, 0, 0), **kw)
    att = jax.ShapeDtypeStruct((n, ATTN_DIM), BF16)
    buf = lambda width: pltpu.VMEM((HALO + PROJ_ROWS, width), F32)
    return pl.pallas_call(
        functools.partial(_in_proj_kernel, tiles_per_seq),
        name="in_proj",
        grid=(n // PROJ_ROWS,),
        in_specs=[
            row(D_MODEL),
            pl.BlockSpec((None, None, 1, D_MODEL), lambda i: (layer, 2, 0, 0)),
            of_layer(D_MODEL, IN_PROJ_WIDTH, pipeline_mode=pl.Buffered(1)),
            of_layer(CONV_WIDTH, CONV_DIM), of_layer(POOL_DIM, POOL_DIM), of_layer(1, POOL_DIM),
        ],
        out_specs=[row(ATTN_DIM), row(ATTN_DIM), row(ATTN_DIM), row(CONV_DIM + POOL_DIM)],
        out_shape=[att, att, att, jax.ShapeDtypeStruct((n, CONV_DIM + POOL_DIM), BF16)],
        scratch_shapes=[buf(CONV_DIM), buf(POOL_DIM), buf(POOL_DIM), buf(POOL_DIM), buf(POOL_DIM)],
        compiler_params=pltpu.CompilerParams(
            dimension_semantics=("arbitrary",), vmem_limit_bytes=VMEM_LIMIT),
    )(x2d, norm_g, w_in, conv_w, pool_w_bd, pool_scale)


def _attn_block(qm, k2, v2, carry, neg_tri, mask):
    z = lax.dot_general(qm, k2, (((1,), (1,)), ((), ())), preferred_element_type=F32)
    neg_abs = lax.bitcast_convert_type(
        lax.bitcast_convert_type(z, jnp.uint32) | jnp.uint32(0x80000000), F32)
    soft = jnp.log(1.0 + jnp.exp(neg_abs))
    drop = jnp.maximum(z, 0.0) + soft
    log_beta = z - drop
    if mask is not None:
        drop = jnp.where(mask, drop, 0.0)
    between = jnp.dot(drop.astype(BF16), neg_tri, preferred_element_type=F32)
    a = jnp.exp(log_beta + between + carry)
    if mask is not None:
        a = jnp.where(mask, a, 0.0)
    out = jnp.dot(a.astype(BF16), v2, preferred_element_type=F32)
    new_carry = carry + between[:, 0:1] - drop[:, 0:1]
    return out, new_carry


def _attn_out_kernel(x_ref, q_ref, k_ref, v_ref, cp_ref, wout_ref, g_ref, o_ref,
                     qm_ref, mix_ref, acc_ref, carry_ref):
    qi = pl.program_id(1)
    n_pairs = ATTN_HEADS // 2
    row = lax.broadcasted_iota(jnp.int32, (ATT_Q, ATT_K), 0)
    col = lax.broadcasted_iota(jnp.int32, (ATT_Q, ATT_K), 1)
    causal = jnp.concatenate([col < row] * 2, axis=0)
    neg_tri = jnp.where(row > col, -1.0, 0.0).astype(BF16)
    lane = lax.broadcasted_iota(jnp.int32, (ATT_Q, LANES), 1)
    first_head = lane < ATTN_HEAD_DIM
    pair_lanes = [slice(hp * LANES, (hp + 1) * LANES) for hp in range(n_pairs)]

    for hp in range(n_pairs):
        q2 = q_ref[0, :, pair_lanes[hp]]
        qm_ref[hp, 0:ATT_Q, :] = jnp.where(first_head, q2, jnp.zeros_like(q2))
        qm_ref[hp, ATT_Q:, :] = jnp.where(first_head, jnp.zeros_like(q2), q2)

    def block(start, mask, is_first):
        for hp in range(n_pairs):
            k2 = k_ref[0, pl.ds(start, ATT_K), pair_lanes[hp]]
            v2 = v_ref[0, pl.ds(start, ATT_K), pair_lanes[hp]]
            carry = jnp.zeros((2 * ATT_Q, 1), F32) if is_first else carry_ref[hp]
            out, carry = _attn_block(qm_ref[hp], k2, v2, carry, neg_tri, mask)
            acc_ref[hp] = out if is_first else acc_ref[hp] + out
            carry_ref[hp] = carry

    block(pl.multiple_of(qi * ATT_K, ATT_K), causal, True)

    def any_weight_left():
        return jnp.max(carry_ref[...]) > UNDERFLOW_LOG

    def body(state):
        step, _ = state
        block(pl.multiple_of((qi - 1 - step) * ATT_K, ATT_K), None, False)
        return step + 1, any_weight_left()

    lax.while_loop(lambda state: jnp.logical_and(state[0] < qi, state[1]), body,
                   (jnp.int32(0), any_weight_left()))

    for hp in range(n_pairs):
        mix_ref[:, pair_lanes[hp]] = jnp.where(
            first_head, acc_ref[hp, 0:ATT_Q, :], acc_ref[hp, ATT_Q:, :])
    mix_ref[:, ATTN_DIM:] = cp_ref[0].astype(F32)
    m = jnp.dot(mix_ref[...], wout_ref[...], preferred_element_type=F32)
    o_ref[0] = x_ref[0] + _rmsnorm(m, g_ref[...])


def _attn_out(x3d, q, k, v, cp, w_out, norm_g, layer):
    b, s, _ = x3d.shape
    tile = lambda width: pl.BlockSpec((1, ATT_Q, width), lambda bi, qi: (bi, qi, 0))
    seq = lambda width: pl.BlockSpec((1, s, width), lambda bi, qi: (bi, 0, 0))
    return pl.pallas_call(
        _attn_out_kernel,
        name="attn_out",
        grid=(b, s // ATT_Q),
        in_specs=[
            tile(D_MODEL), tile(ATTN_DIM), seq(ATTN_DIM), seq(ATTN_DIM), tile(CONV_DIM + POOL_DIM),
            pl.BlockSpec((None, D_MODEL, D_MODEL), lambda bi, qi: (layer, 0, 0)),
            pl.BlockSpec((None, None, 1, D_MODEL), lambda bi, qi: (layer, 3, 0, 0)),
        ],
        out_specs=tile(D_MODEL),
        out_shape=jax.ShapeDtypeStruct(x3d.shape, F32),
        scratch_shapes=[
            pltpu.VMEM((ATTN_HEADS // 2, 2 * ATT_Q, LANES), BF16),
            pltpu.VMEM((ATT_Q, D_MODEL), F32),
            pltpu.VMEM((ATTN_HEADS // 2, 2 * ATT_Q, LANES), F32),
            pltpu.VMEM((ATTN_HEADS // 2, 2 * ATT_Q, 1), F32),
        ],
        compiler_params=pltpu.CompilerParams(
            dimension_semantics=("arbitrary", "arbitrary"), vmem_limit_bytes=VMEM_LIMIT),
    )(x3d, q, k, v, cp, w_out, norm_g)


def _block_diag(pool_w):
    depth, groups, gd, _ = pool_w.shape
    eye = jnp.eye(groups, dtype=pool_w.dtype)
    return jnp.einsum("lgij,gh->lgihj", pool_w, eye).reshape(depth, groups * gd, groups * gd)


def kernel(x, norm_g, ffn_w_gate, ffn_w_up, ffn_w_down, w_in, conv_w, pool_w, pool_scale, w_out):
    b, s, d = x.shape
    assert d == D_MODEL and s % PROJ_ROWS == 0 and s % ATT_Q == 0 and (b * s) % FFN_ROWS == 0
    depth = norm_g.shape[0]
    gains = norm_g.reshape(depth, 6, 1, D_MODEL)
    pool_w_bd = _block_diag(pool_w)
    pool_scale = pool_scale.reshape(depth, 1, POOL_DIM)
    to3d = lambda t: t.reshape(b, s, t.shape[-1])
    x2d = x.reshape(b * s, d)
    for l in range(depth):
        x2d = _ffn(x2d, gains, ffn_w_gate, ffn_w_up, ffn_w_down, l, 0)
        q, k, v, cp = _in_proj(x2d, s, gains, w_in, conv_w, pool_w_bd, pool_scale, l)
        x2d = _attn_out(to3d(x2d), to3d(q), to3d(k), to3d(v), to3d(cp), w_out, gains, l).reshape(b * s, d)
        x2d = _ffn(x2d, gains, ffn_w_gate, ffn_w_up, ffn_w_down, l, 1)
    return x2d.reshape(b, s, d)
```

```python
import functools
import math

import jax
import jax.numpy as jnp
from jax import lax
from jax.experimental import pallas as pl
from jax.experimental.pallas import tpu as pltpu

F32 = jnp.float32
BF16 = jnp.bfloat16

D_MODEL = 1024
D_FF = 2816
ATTN_HEADS = 8
ATTN_HEAD_DIM = 64
ATTN_DIM = ATTN_HEADS * ATTN_HEAD_DIM
CONV_DIM = 256
CONV_WIDTH = 3
POOL_WINDOWS = (2, 4, 8, 16)
POOL_DIM = 256
POOL_GD = POOL_DIM // len(POOL_WINDOWS)
IN_PROJ_WIDTH = 3 * ATTN_DIM + 3 * CONV_DIM + POOL_DIM
RMS_EPS = 1e-6

LANES = 128
FF_CHUNK = 256
N_FF_CHUNKS = D_FF // FF_CHUNK
FFN_ROWS = 512
FFN_GROUP_ROWS = 512
PROJ_ROWS = 1024
PROJ_GROUP_ROWS = 512
HALO = 32
ATT_Q = 256
ATT_K = 256
EARLY_ROWS = 192
UNDERFLOW_LOG = -105.0
VMEM_LIMIT = 56 * 1024 * 1024


def _rmsnorm(x, g):
    ms = jnp.mean(x * x, axis=-1, keepdims=True)
    return x * lax.rsqrt(ms + RMS_EPS) * g


def _ffn_kernel(x_ref, gpre_ref, gpost_ref, wg_ref, wu_ref, wd_ref, o_ref):
    groups = [slice(r, r + FFN_GROUP_ROWS) for r in range(0, FFN_ROWS, FFN_GROUP_ROWS)]
    hs = [_rmsnorm(x_ref[rows, :], gpre_ref[...]) for rows in groups]
    accs = [jnp.zeros((FFN_GROUP_ROWS, D_MODEL), F32) for _ in groups]
    for c in range(N_FF_CHUNKS):
        cols = slice(c * FF_CHUNK, (c + 1) * FF_CHUNK)
        for i, h in enumerate(hs):
            g = jnp.dot(h, wg_ref[:, cols], preferred_element_type=F32)
            u = jnp.dot(h, wu_ref[:, cols], preferred_element_type=F32)
            a = g * jax.nn.sigmoid(g) * u
            accs[i] = accs[i] + jnp.dot(a, wd_ref[cols, :], preferred_element_type=F32)
    for rows, acc in zip(groups, accs):
        o_ref[rows, :] = x_ref[rows, :] + 0.5 * _rmsnorm(acc, gpost_ref[...])


def _ffn(x2d, norm_g, w_gate, w_up, w_down, layer, half):
    n = x2d.shape[0]
    row_spec = pl.BlockSpec((FFN_ROWS, D_MODEL), lambda i: (i, 0))
    gain = lambda which: pl.BlockSpec((None, None, 1, D_MODEL), lambda i: (layer, which, 0, 0))
    resident = lambda rows, cols: pl.BlockSpec(
        (None, None, rows, cols), lambda i: (layer, half, 0, 0), pipeline_mode=pl.Buffered(1))
    pre, post = (0, 1) if half == 0 else (4, 5)
    return pl.pallas_call(
        _ffn_kernel,
        name="ffn",
        grid=(n // FFN_ROWS,),
        in_specs=[
            row_spec, gain(pre), gain(post),
            resident(D_MODEL, D_FF), resident(D_MODEL, D_FF), resident(D_FF, D_MODEL),
        ],
        out_specs=row_spec,
        out_shape=jax.ShapeDtypeStruct(x2d.shape, F32),
        compiler_params=pltpu.CompilerParams(
            dimension_semantics=("arbitrary",), vmem_limit_bytes=VMEM_LIMIT),
    )(x2d, norm_g, norm_g, w_gate, w_up, w_down)


def _in_proj_kernel(tiles_per_seq, x_ref, g_ref, w_ref, convw_ref, poolw_ref, pscale_ref,
                    q_ref, k_ref, v_ref, cp_ref, ubuf, pbuf, s2buf, s4buf, s8buf):
    tile_in_seq = pl.program_id(0) % tiles_per_seq

    @pl.when(tile_in_seq == 0)
    def _():
        ubuf[0:HALO, :] = jnp.zeros((HALO, CONV_DIM), F32)
        pbuf[0:HALO, :] = jnp.zeros((HALO, POOL_DIM), F32)

    @pl.when(tile_in_seq != 0)
    def _():
        ubuf[0:HALO, :] = ubuf[PROJ_ROWS:PROJ_ROWS + HALO, :]
        pbuf[0:HALO, :] = pbuf[PROJ_ROWS:PROJ_ROWS + HALO, :]

    starts = list(range(0, PROJ_ROWS, PROJ_GROUP_ROWS))
    mix_cols = slice(3 * ATTN_DIM, IN_PROJ_WIDTH)
    h = _rmsnorm(x_ref[0:PROJ_GROUP_ROWS, :], g_ref[...])
    for gi, r0 in enumerate(starts):
        out_rows = slice(r0, r0 + PROJ_GROUP_ROWS)
        mix_in = jnp.dot(h, w_ref[:, mix_cols], preferred_element_type=F32)
        q = jnp.dot(h, w_ref[:, 0:ATTN_DIM], preferred_element_type=F32)
        q_ref[out_rows, :] = (q * (1.0 / math.sqrt(ATTN_HEAD_DIM))).astype(BF16)
        _mix_rows(r0, tile_in_seq * PROJ_ROWS + r0, mix_in, convw_ref, poolw_ref, pscale_ref,
                  cp_ref, ubuf, pbuf, s2buf, s4buf, s8buf)
        k_ref[out_rows, :] = jnp.dot(h, w_ref[:, ATTN_DIM:2 * ATTN_DIM],
                                     preferred_element_type=F32).astype(BF16)
        h_next = None
        if gi + 1 < len(starts):
            nxt = starts[gi + 1]
            h_next = _rmsnorm(x_ref[nxt:nxt + PROJ_GROUP_ROWS, :], g_ref[...])
        v_ref[out_rows, :] = jnp.dot(h, w_ref[:, 2 * ATTN_DIM:3 * ATTN_DIM],
                                     preferred_element_type=F32).astype(BF16)
        h = h_next


def _mix_rows(r0, seq_pos, mix_in, convw_ref, poolw_ref, pscale_ref,
              cp_ref, ubuf, pbuf, s2buf, s4buf, s8buf):
    rows = PROJ_GROUP_ROWS
    out_rows = slice(r0, r0 + rows)
    gate_b = mix_in[:, 0:CONV_DIM]
    gate_c = mix_in[:, CONV_DIM:2 * CONV_DIM]
    conv_h = mix_in[:, 2 * CONV_DIM:3 * CONV_DIM]
    p = mix_in[:, 3 * CONV_DIM:]

    tok = r0 + HALO
    end = tok + rows
    ubuf[tok:end, :] = gate_c * conv_h
    cw = convw_ref[...]
    y = (cw[0:1, :] * ubuf[tok - 2:end - 2, :]
         + cw[1:2, :] * ubuf[tok - 1:end - 1, :]
         + cw[2:3, :] * ubuf[tok:end, :])
    cp_ref[out_rows, 0:CONV_DIM] = (gate_b * y).astype(BF16)

    pbuf[tok:end, :] = p
    s2buf[r0 + 8:end, :] = pbuf[r0 + 8:end, :] + pbuf[r0 + 7:end - 1, :]
    s4buf[r0 + 16:end, :] = s2buf[r0 + 16:end, :] + s2buf[r0 + 14:end - 2, :]
    s8buf[r0 + 24:end, :] = s4buf[r0 + 24:end, :] + s4buf[r0 + 20:end - 4, :]
    s16 = s8buf[tok:end, :] + s8buf[tok - 8:end - 8, :]
    group = lax.broadcasted_iota(jnp.int32, (rows, POOL_DIM), 1) // POOL_GD
    win_sum = jnp.where(group == 0, s2buf[tok:end, :],
                        jnp.where(group == 1, s4buf[tok:end, :],
                                  jnp.where(group == 2, s8buf[tok:end, :], s16)))
    window = jnp.where(group == 0, POOL_WINDOWS[0],
                       jnp.where(group == 1, POOL_WINDOWS[1],
                                 jnp.where(group == 2, POOL_WINDOWS[2], POOL_WINDOWS[3])))
    pos = seq_pos + lax.broadcasted_iota(jnp.int32, (rows, POOL_DIM), 0)
    count = jnp.minimum(pos + 1, window).astype(F32)
    d = win_sum / count - p
    pooled = jnp.dot(d, poolw_ref[...], preferred_element_type=F32)
    cp_ref[out_rows, CONV_DIM:] = (pooled * pscale_ref[...]).astype(BF16)


def _in_proj(x2d, seq, norm_g, w_in, conv_w, pool_w_bd, pool_scale, layer):
    n = x2d.shape[0]
    tiles_per_seq = seq // PROJ_ROWS
    row = lambda width: pl.BlockSpec((PROJ_ROWS, width), lambda i: (i, 0))
    of_layer = lambda rows, cols, **kw: pl.BlockSpec((None, rows, cols), lambda i: (layer, 0, 0), **kw)
    att = jax.ShapeDtypeStruct((n, ATTN_DIM), BF16)
    buf = lambda width: pltpu.VMEM((HALO + PROJ_ROWS, width), F32)
    return pl.pallas_call(
        functools.partial(_in_proj_kernel, tiles_per_seq),
        name="in_proj",
        grid=(n // PROJ_ROWS,),
        in_specs=[
            row(D_MODEL),
            pl.BlockSpec((None, None, 1, D_MODEL), lambda i: (layer, 2, 0, 0)),
            of_layer(D_MODEL, IN_PROJ_WIDTH, pipeline_mode=pl.Buffered(1)),
            of_layer(CONV_WIDTH, CONV_DIM), of_layer(POOL_DIM, POOL_DIM), of_layer(1, POOL_DIM),
        ],
        out_specs=[row(ATTN_DIM), row(ATTN_DIM), row(ATTN_DIM), row(CONV_DIM + POOL_DIM)],
        out_shape=[att, att, att, jax.ShapeDtypeStruct((n, CONV_DIM + POOL_DIM), BF16)],
        scratch_shapes=[buf(CONV_DIM), buf(POOL_DIM), buf(POOL_DIM), buf(POOL_DIM), buf(POOL_DIM)],
        compiler_params=pltpu.CompilerParams(
            dimension_semantics=("arbitrary",), vmem_limit_bytes=VMEM_LIMIT),
    )(x2d, norm_g, w_in, conv_w, pool_w_bd, pool_scale)


def _attn_block(qm, k2, v2, carry, neg_tri, mask):
    z = lax.dot_general(qm, k2, (((1,), (1,)), ((), ())), preferred_element_type=F32)
    neg_abs = lax.bitcast_convert_type(
        lax.bitcast_convert_type(z, jnp.uint32) | jnp.uint32(0x80000000), F32)
    soft = jnp.log(1.0 + jnp.exp(neg_abs))
    drop = jnp.maximum(z, 0.0) + soft
    log_beta = z - drop
    if mask is not None:
        drop = jnp.where(mask, drop, 0.0)
    between = jnp.dot(drop.astype(BF16), neg_tri, preferred_element_type=F32)
    a = jnp.exp(log_beta + between + carry)
    if mask is not None:
        a = jnp.where(mask, a, 0.0)
    out = jnp.dot(a.astype(BF16), v2, preferred_element_type=F32)
    new_carry = carry + between[:, 0:1] - drop[:, 0:1]
    return out, new_carry


def _attn_out_kernel(x_ref, q_ref, k_ref, v_ref, cp_ref, wout_ref, g_ref, o_ref,
                     qm_ref, mix_ref, acc_ref, carry_ref):
    qi = pl.program_id(1)
    n_pairs = ATTN_HEADS // 2
    early, late = slice(0, EARLY_ROWS), slice(EARLY_ROWS, ATT_Q)
    n_early = 2 * EARLY_ROWS
    row = lax.broadcasted_iota(jnp.int32, (ATT_Q, ATT_K), 0)
    col = lax.broadcasted_iota(jnp.int32, (ATT_Q, ATT_K), 1)
    neg_tri = jnp.where(row > col, -1.0, 0.0).astype(BF16)

    rows_early = lax.broadcasted_iota(jnp.int32, (EARLY_ROWS, ATT_K), 0)
    rows_late = EARLY_ROWS + lax.broadcasted_iota(jnp.int32, (ATT_Q - EARLY_ROWS, ATT_K), 0)
    tile_row = jnp.concatenate([rows_early, rows_early, rows_late, rows_late], axis=0)
    causal = lax.broadcasted_iota(jnp.int32, (2 * ATT_Q, ATT_K), 1) < tile_row
    first_head = lax.broadcasted_iota(jnp.int32, (ATT_Q, LANES), 1) < ATTN_HEAD_DIM
    pair_lanes = [slice(hp * LANES, (hp + 1) * LANES) for hp in range(n_pairs)]

    for hp in range(n_pairs):
        q2 = q_ref[0, :, pair_lanes[hp]]
        head0 = jnp.where(first_head, q2, jnp.zeros_like(q2))
        head1 = jnp.where(first_head, jnp.zeros_like(q2), q2)
        qm_ref[hp] = jnp.concatenate([head0[early], head1[early], head0[late], head1[late]], axis=0)

    def block(hp, blocks_back, rows, carry, mask):
        start = pl.multiple_of((qi - blocks_back) * ATT_K, ATT_K)
        k2 = k_ref[0, pl.ds(start, ATT_K), pair_lanes[hp]]
        v2 = v_ref[0, pl.ds(start, ATT_K), pair_lanes[hp]]
        return _attn_block(qm_ref[hp, rows, :], k2, v2, carry, neg_tri, mask)

    everything, early_rows, late_rows = slice(0, 2 * ATT_Q), slice(0, n_early), slice(n_early, 2 * ATT_Q)

    def leading_blocks(with_previous):
        for hp in range(n_pairs):
            out, carry = block(hp, 0, everything, jnp.zeros((2 * ATT_Q, 1), F32), causal)
            acc_ref[hp] = out
            carry_ref[hp] = carry
            if with_previous:
                out_prev, carry_prev = block(hp, 1, early_rows, carry_ref[hp, early_rows, :], None)
                acc_ref[hp, early_rows, :] += out_prev
                carry_ref[hp, early_rows, :] = carry_prev

    pl.when(qi == 0)(lambda: leading_blocks(False))
    pl.when(qi > 0)(lambda: leading_blocks(True))

    def weight_left(rows):
        return jnp.max(carry_ref[:, rows, :]) > UNDERFLOW_LOG

    def accumulate(blocks_back, rows):
        for hp in range(n_pairs):
            out, carry = block(hp, blocks_back, rows, carry_ref[hp, rows, :], None)
            acc_ref[hp, rows, :] += out
            carry_ref[hp, rows, :] = carry

    late_left = weight_left(late_rows)
    early_left = weight_left(early_rows)
    pl.when(jnp.logical_and(qi > 0, late_left))(lambda: accumulate(1, late_rows))

    def body(state):
        blocks_back, _ = state
        accumulate(blocks_back, everything)
        return blocks_back + 1, weight_left(everything)

    lax.while_loop(lambda state: jnp.logical_and(state[0] <= qi, state[1]), body,
                   (jnp.int32(2), jnp.logical_or(early_left, late_left)))

    for hp in range(n_pairs):
        for rows, first, second in ((early, slice(0, EARLY_ROWS), slice(EARLY_ROWS, n_early)),
                                    (late, slice(n_early, n_early + ATT_Q - EARLY_ROWS),
                                     slice(n_early + ATT_Q - EARLY_ROWS, 2 * ATT_Q))):
            head0_lanes = lax.broadcasted_iota(jnp.int32, (rows.stop - rows.start, LANES), 1) < ATTN_HEAD_DIM
            mix_ref[rows, pair_lanes[hp]] = jnp.where(head0_lanes, acc_ref[hp, first, :], acc_ref[hp, second, :])
    mix_ref[:, ATTN_DIM:] = cp_ref[0].astype(F32)
    m = jnp.dot(mix_ref[...], wout_ref[...], preferred_element_type=F32)
    o_ref[0] = x_ref[0] + _rmsnorm(m, g_ref[...])


def _attn_out(x3d, q, k, v, cp, w_out, norm_g, layer):
    b, s, _ = x3d.shape
    tile = lambda width: pl.BlockSpec((1, ATT_Q, width), lambda bi, qi: (bi, qi, 0))
    seq = lambda width: pl.BlockSpec((1, s, width), lambda bi, qi: (bi, 0, 0))
    return pl.pallas_call(
        _attn_out_kernel,
        name="attn_out",
        grid=(b, s // ATT_Q),
        in_specs=[
            tile(D_MODEL), tile(ATTN_DIM), seq(ATTN_DIM), seq(ATTN_DIM), tile(CONV_DIM + POOL_DIM),
            pl.BlockSpec((None, D_MODEL, D_MODEL), lambda bi, qi: (layer, 0, 0)),
            pl.BlockSpec((None, None, 1, D_MODEL), lambda bi, qi: (layer, 3, 0, 0)),
        ],
        out_specs=tile(D_MODEL),
        out_shape=jax.ShapeDtypeStruct(x3d.shape, F32),
        scratch_shapes=[
            pltpu.VMEM((ATTN_HEADS // 2, 2 * ATT_Q, LANES), BF16),
            pltpu.VMEM((ATT_Q, D_MODEL), F32),
            pltpu.VMEM((ATTN_HEADS // 2, 2 * ATT_Q, LANES), F32),
            pltpu.VMEM((ATTN_HEADS // 2, 2 * ATT_Q, 1), F32),
        ],
        compiler_params=pltpu.CompilerParams(
            dimension_semantics=("arbitrary", "arbitrary"), vmem_limit_bytes=VMEM_LIMIT),
    )(x3d, q, k, v, cp, w_out, norm_g)


def _block_diag(pool_w):
    depth, groups, gd, _ = pool_w.shape
    eye = jnp.eye(groups, dtype=pool_w.dtype)
    return jnp.einsum("lgij,gh->lgihj", pool_w, eye).reshape(depth, groups * gd, groups * gd)


def kernel(x, norm_g, ffn_w_gate, ffn_w_up, ffn_w_down, w_in, conv_w, pool_w, pool_scale, w_out):
    b, s, d = x.shape
    assert d == D_MODEL and s % PROJ_ROWS == 0 and s % ATT_Q == 0 and (b * s) % FFN_ROWS == 0
    depth = norm_g.shape[0]
    gains = norm_g.reshape(depth, 6, 1, D_MODEL)
    pool_w_bd = _block_diag(pool_w)
    pool_scale = pool_scale.reshape(depth, 1, POOL_DIM)
    to3d = lambda t: t.reshape(b, s, t.shape[-1])
    x2d = x.reshape(b * s, d)
    for l in range(depth):
        x2d = _ffn(x2d, gains, ffn_w_gate, ffn_w_up, ffn_w_down, l, 0)
        q, k, v, cp = _in_proj(x2d, s, gains, w_in, conv_w, pool_w_bd, pool_scale, l)
        x2d = _attn_out(to3d(x2d), to3d(q), to3d(k), to3d(v), to3d(cp), w_out, gains, l).reshape(b * s, d)
        x2d = _ffn(x2d, gains, ffn_w_gate, ffn_w_up, ffn_w_down, l, 1)
    return x2d.reshape(b, s, d)
```

```python
import functools
import math

import jax
import jax.numpy as jnp
from jax import lax
from jax.experimental import pallas as pl
from jax.experimental.pallas import tpu as pltpu

F32 = jnp.float32
BF16 = jnp.bfloat16

D_MODEL = 1024
D_FF = 2816
ATTN_HEADS = 8
ATTN_HEAD_DIM = 64
ATTN_DIM = ATTN_HEADS * ATTN_HEAD_DIM
CONV_DIM = 256
CONV_WIDTH = 3
POOL_WINDOWS = (2, 4, 8, 16)
POOL_DIM = 256
POOL_GD = POOL_DIM // len(POOL_WINDOWS)
IN_PROJ_WIDTH = 3 * ATTN_DIM + 3 * CONV_DIM + POOL_DIM
RMS_EPS = 1e-6

LANES = 128
FF_CHUNK = 256
N_FF_CHUNKS = D_FF // FF_CHUNK
FFN_ROWS = 512
FFN_GROUP_ROWS = 512
PROJ_ROWS = 1024
PROJ_GROUP_ROWS = 512
HALO = 32
ATT_Q = 256
ATT_K = 256
EARLY_ROWS = 192
UNDERFLOW_LOG = -105.0
VMEM_LIMIT = 56 * 1024 * 1024


def _rmsnorm(x, g):
    ms = jnp.mean(x * x, axis=-1, keepdims=True)
    return x * lax.rsqrt(ms + RMS_EPS) * g


def _ffn_kernel(x_ref, gpre_ref, gpost_ref, wg_ref, wu_ref, wd_ref, o_ref):
    groups = [slice(r, r + FFN_GROUP_ROWS) for r in range(0, FFN_ROWS, FFN_GROUP_ROWS)]
    hs = [_rmsnorm(x_ref[rows, :], gpre_ref[...]) for rows in groups]
    accs = [jnp.zeros((FFN_GROUP_ROWS, D_MODEL), F32) for _ in groups]
    for c in range(N_FF_CHUNKS):
        cols = slice(c * FF_CHUNK, (c + 1) * FF_CHUNK)
        for i, h in enumerate(hs):
            g = jnp.dot(h, wg_ref[:, cols], preferred_element_type=F32)
            u = jnp.dot(h, wu_ref[:, cols], preferred_element_type=F32)
            a = g * jax.nn.sigmoid(g) * u
            accs[i] = accs[i] + jnp.dot(a, wd_ref[cols, :], preferred_element_type=F32)
    for rows, acc in zip(groups, accs):
        o_ref[rows, :] = x_ref[rows, :] + 0.5 * _rmsnorm(acc, gpost_ref[...])


def _ffn(x2d, norm_g, w_gate, w_up, w_down, layer, half):
    n = x2d.shape[0]
    row_spec = pl.BlockSpec((FFN_ROWS, D_MODEL), lambda i: (i, 0))
    gain = lambda which: pl.BlockSpec((None, None, 1, D_MODEL), lambda i: (layer, which, 0, 0))
    resident = lambda rows, cols: pl.BlockSpec(
        (None, None, rows, cols), lambda i: (layer, half, 0, 0), pipeline_mode=pl.Buffered(1))
    pre, post = (0, 1) if half == 0 else (4, 5)
    return pl.pallas_call(
        _ffn_kernel,
        name="ffn",
        grid=(n // FFN_ROWS,),
        in_specs=[
            row_spec, gain(pre), gain(post),
            resident(D_MODEL, D_FF), resident(D_MODEL, D_FF), resident(D_FF, D_MODEL),
        ],
        out_specs=row_spec,
        out_shape=jax.ShapeDtypeStruct(x2d.shape, F32),
        compiler_params=pltpu.CompilerParams(
            dimension_semantics=("arbitrary",), vmem_limit_bytes=VMEM_LIMIT),
    )(x2d, norm_g, norm_g, w_gate, w_up, w_down)


def _in_proj_kernel(tiles_per_seq, x_ref, g_ref, w_ref, convw_ref, poolw_ref, pscale_ref,
                    q_ref, k_ref, v_ref, cp_ref, ubuf, pbuf, s2buf, s4buf, s8buf):
    tile_in_seq = pl.program_id(0) % tiles_per_seq

    @pl.when(tile_in_seq == 0)
    def _():
        ubuf[0:HALO, :] = jnp.zeros((HALO, CONV_DIM), F32)
        pbuf[0:HALO, :] = jnp.zeros((HALO, POOL_DIM), F32)

    @pl.when(tile_in_seq != 0)
    def _():
        ubuf[0:HALO, :] = ubuf[PROJ_ROWS:PROJ_ROWS + HALO, :]
        pbuf[0:HALO, :] = pbuf[PROJ_ROWS:PROJ_ROWS + HALO, :]

    starts = list(range(0, PROJ_ROWS, PROJ_GROUP_ROWS))
    mix_cols = slice(3 * ATTN_DIM, IN_PROJ_WIDTH)
    h = _rmsnorm(x_ref[0:PROJ_GROUP_ROWS, :], g_ref[...])
    for gi, r0 in enumerate(starts):
        out_rows = slice(r0, r0 + PROJ_GROUP_ROWS)
        mix_in = jnp.dot(h, w_ref[:, mix_cols], preferred_element_type=F32)
        q = jnp.dot(h, w_ref[:, 0:ATTN_DIM], preferred_element_type=F32)
        q_ref[out_rows, :] = (q * (1.0 / math.sqrt(ATTN_HEAD_DIM))).astype(BF16)
        _mix_rows(r0, tile_in_seq * PROJ_ROWS + r0, mix_in, convw_ref, poolw_ref, pscale_ref,
                  cp_ref, ubuf, pbuf, s2buf, s4buf, s8buf)
        k_ref[out_rows, :] = jnp.dot(h, w_ref[:, ATTN_DIM:2 * ATTN_DIM],
                                     preferred_element_type=F32).astype(BF16)
        h_next = None
        if gi + 1 < len(starts):
            nxt = starts[gi + 1]
            h_next = _rmsnorm(x_ref[nxt:nxt + PROJ_GROUP_ROWS, :], g_ref[...])
        v_ref[out_rows, :] = jnp.dot(h, w_ref[:, 2 * ATTN_DIM:3 * ATTN_DIM],
                                     preferred_element_type=F32).astype(BF16)
        h = h_next


def _mix_rows(r0, seq_pos, mix_in, convw_ref, poolw_ref, pscale_ref,
              cp_ref, ubuf, pbuf, s2buf, s4buf, s8buf):
    rows = PROJ_GROUP_ROWS
    out_rows = slice(r0, r0 + rows)
    gate_b = mix_in[:, 0:CONV_DIM]
    gate_c = mix_in[:, CONV_DIM:2 * CONV_DIM]
    conv_h = mix_in[:, 2 * CONV_DIM:3 * CONV_DIM]
    p = mix_in[:, 3 * CONV_DIM:]

    tok = r0 + HALO
    end = tok + rows
    ubuf[tok:end, :] = gate_c * conv_h
    cw = convw_ref[...]
    y = (cw[0:1, :] * ubuf[tok - 2:end - 2, :]
         + cw[1:2, :] * ubuf[tok - 1:end - 1, :]
         + cw[2:3, :] * ubuf[tok:end, :])
    cp_ref[out_rows, 0:CONV_DIM] = (gate_b * y).astype(BF16)

    pbuf[tok:end, :] = p
    s2buf[r0 + 8:end, :] = pbuf[r0 + 8:end, :] + pbuf[r0 + 7:end - 1, :]
    s4buf[r0 + 16:end, :] = s2buf[r0 + 16:end, :] + s2buf[r0 + 14:end - 2, :]
    s8buf[r0 + 24:end, :] = s4buf[r0 + 24:end, :] + s4buf[r0 + 20:end - 4, :]
    s16 = s8buf[tok:end, :] + s8buf[tok - 8:end - 8, :]
    group = lax.broadcasted_iota(jnp.int32, (rows, POOL_DIM), 1) // POOL_GD
    win_sum = jnp.where(group == 0, s2buf[tok:end, :],
                        jnp.where(group == 1, s4buf[tok:end, :],
                                  jnp.where(group == 2, s8buf[tok:end, :], s16)))
    window = jnp.where(group == 0, POOL_WINDOWS[0],
                       jnp.where(group == 1, POOL_WINDOWS[1],
                                 jnp.where(group == 2, POOL_WINDOWS[2], POOL_WINDOWS[3])))
    pos = seq_pos + lax.broadcasted_iota(jnp.int32, (rows, POOL_DIM), 0)
    count = jnp.minimum(pos + 1, window).astype(F32)
    d = win_sum / count - p
    pooled = jnp.dot(d, poolw_ref[...], preferred_element_type=F32)
    cp_ref[out_rows, CONV_DIM:] = (pooled * pscale_ref[...]).astype(BF16)


def _in_proj(x2d, seq, norm_g, w_in, conv_w, pool_w_bd, pool_scale, layer):
    n = x2d.shape[0]
    tiles_per_seq = seq // PROJ_ROWS
    row = lambda width: pl.BlockSpec((PROJ_ROWS, width), lambda i: (i, 0))
    of_layer = lambda rows, cols, **kw: pl.BlockSpec((None, rows, cols), lambda i: (layer, 0, 0), **kw)
    att = jax.ShapeDtypeStruct((n, ATTN_DIM), BF16)
    buf = lambda width: pltpu.VMEM((HALO + PROJ_ROWS, width), F32)
    return pl.pallas_call(
        functools.partial(_in_proj_kernel, tiles_per_seq),
        name="in_proj",
        grid=(n // PROJ_ROWS,),
        in_specs=[
            row(D_MODEL),
            pl.BlockSpec((None, None, 1, D_MODEL), lambda i: (layer, 2, 0, 0)),
            of_layer(D_MODEL, IN_PROJ_WIDTH, pipeline_mode=pl.Buffered(1)),
            of_layer(CONV_WIDTH, CONV_DIM), of_layer(POOL_DIM, POOL_DIM), of_layer(1, POOL_DIM),
        ],
        out_specs=[row(ATTN_DIM), row(ATTN_DIM), row(ATTN_DIM), row(CONV_DIM + POOL_DIM)],
        out_shape=[att, att, att, jax.ShapeDtypeStruct((n, CONV_DIM + POOL_DIM), BF16)],
        scratch_shapes=[buf(CONV_DIM), buf(POOL_DIM), buf(POOL_DIM), buf(POOL_DIM), buf(POOL_DIM)],
        compiler_params=pltpu.CompilerParams(
            dimension_semantics=("arbitrary",), vmem_limit_bytes=VMEM_LIMIT),
    )(x2d, norm_g, w_in, conv_w, pool_w_bd, pool_scale)


def _attn_scores(qm, k2):
    return lax.dot_general(qm, k2, (((1,), (1,)), ((), ())), preferred_element_type=F32)


def _attn_block(z, v2, carry, neg_tri, mask):
    neg_abs = lax.bitcast_convert_type(
        lax.bitcast_convert_type(z, jnp.uint32) | jnp.uint32(0x80000000), F32)
    soft = jnp.log(1.0 + jnp.exp(neg_abs))
    drop = jnp.maximum(z, 0.0) + soft
    log_beta = z - drop
    if mask is not None:
        drop = jnp.where(mask, drop, 0.0)
    between = jnp.dot(drop.astype(BF16), neg_tri, preferred_element_type=F32)
    a = jnp.exp(log_beta + between + carry)
    if mask is not None:
        a = jnp.where(mask, a, 0.0)
    out = jnp.dot(a.astype(BF16), v2, preferred_element_type=F32)
    new_carry = carry + between[:, 0:1] - drop[:, 0:1]
    return out, new_carry


def _attn_out_kernel(x_ref, q_ref, k_ref, v_ref, cp_ref, wout_ref, g_ref, o_ref,
                     qm_ref, mix_ref, acc_ref, carry_ref):
    qi = pl.program_id(1)
    n_pairs = ATTN_HEADS // 2
    early, late = slice(0, EARLY_ROWS), slice(EARLY_ROWS, ATT_Q)
    n_early = 2 * EARLY_ROWS
    row = lax.broadcasted_iota(jnp.int32, (ATT_Q, ATT_K), 0)
    col = lax.broadcasted_iota(jnp.int32, (ATT_Q, ATT_K), 1)
    neg_tri = jnp.where(row > col, -1.0, 0.0).astype(BF16)

    rows_early = lax.broadcasted_iota(jnp.int32, (EARLY_ROWS, ATT_K), 0)
    rows_late = EARLY_ROWS + lax.broadcasted_iota(jnp.int32, (ATT_Q - EARLY_ROWS, ATT_K), 0)
    tile_row = jnp.concatenate([rows_early, rows_early, rows_late, rows_late], axis=0)
    causal = lax.broadcasted_iota(jnp.int32, (2 * ATT_Q, ATT_K), 1) < tile_row
    first_head = lax.broadcasted_iota(jnp.int32, (ATT_Q, LANES), 1) < ATTN_HEAD_DIM
    pair_lanes = [slice(hp * LANES, (hp + 1) * LANES) for hp in range(n_pairs)]

    for hp in range(n_pairs):
        q2 = q_ref[0, :, pair_lanes[hp]]
        head0 = jnp.where(first_head, q2, jnp.zeros_like(q2))
        head1 = jnp.where(first_head, jnp.zeros_like(q2), q2)
        qm_ref[hp] = jnp.concatenate([head0[early], head1[early], head0[late], head1[late]], axis=0)

    def scores(hp, blocks_back, rows):
        start = pl.multiple_of((qi - blocks_back) * ATT_K, ATT_K)
        return _attn_scores(qm_ref[hp, rows, :], k_ref[0, pl.ds(start, ATT_K), pair_lanes[hp]])

    def sweep(items):
        z = scores(*items[0][:3])
        for n, (hp, blocks_back, rows, is_diagonal) in enumerate(items):
            z_next = scores(*items[n + 1][:3]) if n + 1 < len(items) else None
            start = pl.multiple_of((qi - blocks_back) * ATT_K, ATT_K)
            v2 = v_ref[0, pl.ds(start, ATT_K), pair_lanes[hp]]
            if is_diagonal:
                out, carry = _attn_block(z, v2, jnp.zeros((2 * ATT_Q, 1), F32), neg_tri, causal)
                acc_ref[hp, rows, :] = out
            else:
                out, carry = _attn_block(z, v2, carry_ref[hp, rows, :], neg_tri, None)
                acc_ref[hp, rows, :] += out
            carry_ref[hp, rows, :] = carry
            z = z_next

    everything, early_rows, late_rows = slice(0, 2 * ATT_Q), slice(0, n_early), slice(n_early, 2 * ATT_Q)

    def leading_blocks(with_previous):
        items = []
        for hp in range(n_pairs):
            items.append((hp, 0, everything, True))
            if with_previous:
                items.append((hp, 1, early_rows, False))
        sweep(items)

    pl.when(qi == 0)(lambda: leading_blocks(False))
    pl.when(qi > 0)(lambda: leading_blocks(True))

    def weight_left(rows):
        return jnp.max(carry_ref[:, rows, :]) > UNDERFLOW_LOG

    def accumulate(blocks_back, rows):
        sweep([(hp, blocks_back, rows, False) for hp in range(n_pairs)])

    late_left = weight_left(late_rows)
    early_left = weight_left(early_rows)
    pl.when(jnp.logical_and(qi > 0, late_left))(lambda: accumulate(1, late_rows))

    def body(state):
        blocks_back, _ = state
        accumulate(blocks_back, everything)
        return blocks_back + 1, weight_left(everything)

    lax.while_loop(lambda state: jnp.logical_and(state[0] <= qi, state[1]), body,
                   (jnp.int32(2), jnp.logical_or(early_left, late_left)))

    for hp in range(n_pairs):
        for rows, first, second in ((early, slice(0, EARLY_ROWS), slice(EARLY_ROWS, n_early)),
                                    (late, slice(n_early, n_early + ATT_Q - EARLY_ROWS),
                                     slice(n_early + ATT_Q - EARLY_ROWS, 2 * ATT_Q))):
            head0_lanes = lax.broadcasted_iota(jnp.int32, (rows.stop - rows.start, LANES), 1) < ATTN_HEAD_DIM
            mix_ref[rows, pair_lanes[hp]] = jnp.where(head0_lanes, acc_ref[hp, first, :], acc_ref[hp, second, :])
    mix_ref[:, ATTN_DIM:] = cp_ref[0].astype(F32)
    m = jnp.dot(mix_ref[...], wout_ref[...], preferred_element_type=F32)
    o_ref[0] = x_ref[0] + _rmsnorm(m, g_ref[...])


def _attn_out(x3d, q, k, v, cp, w_out, norm_g, layer):
    b, s, _ = x3d.shape
    tile = lambda width: pl.BlockSpec((1, ATT_Q, width), lambda bi, qi: (bi, qi, 0))
    seq = lambda width: pl.BlockSpec((1, s, width), lambda bi, qi: (bi, 0, 0))
    return pl.pallas_call(
        _attn_out_kernel,
        name="attn_out",
        grid=(b, s // ATT_Q),
        in_specs=[
            tile(D_MODEL), tile(ATTN_DIM), seq(ATTN_DIM), seq(ATTN_DIM), tile(CONV_DIM + POOL_DIM),
            pl.BlockSpec((None, D_MODEL, D_MODEL), lambda bi, qi: (layer, 0, 0)),
            pl.BlockSpec((None, None, 1, D_MODEL), lambda bi, qi: (layer, 3, 0, 0)),
        ],
        out_specs=tile(D_MODEL),
        out_shape=jax.ShapeDtypeStruct(x3d.shape, F32),
        scratch_shapes=[
            pltpu.VMEM((ATTN_HEADS // 2, 2 * ATT_Q, LANES), BF16),
            pltpu.VMEM((ATT_Q, D_MODEL), F32),
            pltpu.VMEM((ATTN_HEADS // 2, 2 * ATT_Q, LANES), F32),
            pltpu.VMEM((ATTN_HEADS // 2, 2 * ATT_Q, 1), F32),
        ],
        compiler_params=pltpu.CompilerParams(
            dimension_semantics=("arbitrary", "arbitrary"), vmem_limit_bytes=VMEM_LIMIT),
    )(x3d, q, k, v, cp, w_out, norm_g)


def _block_diag(pool_w):
    depth, groups, gd, _ = pool_w.shape
    eye = jnp.eye(groups, dtype=pool_w.dtype)
    return jnp.einsum("lgij,gh->lgihj", pool_w, eye).reshape(depth, groups * gd, groups * gd)


def kernel(x, norm_g, ffn_w_gate, ffn_w_up, ffn_w_down, w_in, conv_w, pool_w, pool_scale, w_out):
    b, s, d = x.shape
    assert d == D_MODEL and s % PROJ_ROWS == 0 and s % ATT_Q == 0 and (b * s) % FFN_ROWS == 0
    depth = norm_g.shape[0]
    gains = norm_g.reshape(depth, 6, 1, D_MODEL)
    pool_w_bd = _block_diag(pool_w)
    pool_scale = pool_scale.reshape(depth, 1, POOL_DIM)
    to3d = lambda t: t.reshape(b, s, t.shape[-1])
    x2d = x.reshape(b * s, d)
    for l in range(depth):
        x2d = _ffn(x2d, gains, ffn_w_gate, ffn_w_up, ffn_w_down, l, 0)
        q, k, v, cp = _in_proj(x2d, s, gains, w_in, conv_w, pool_w_bd, pool_scale, l)
        x2d = _attn_out(to3d(x2d), to3d(q), to3d(k), to3d(v), to3d(cp), w_out, gains, l).reshape(b * s, d)
        x2d = _ffn(x2d, gains, ffn_w_gate, ffn_w_up, ffn_w_down, l, 1)
    return x2d.reshape(b, s, d)
```

```python
import functools
import math

import jax
import jax.numpy as jnp
from jax import lax
from jax.experimental import pallas as pl
from jax.experimental.pallas import tpu as pltpu

F32 = jnp.float32
BF16 = jnp.bfloat16

D_MODEL = 1024
D_FF = 2816
ATTN_HEADS = 8
ATTN_HEAD_DIM = 64
ATTN_DIM = ATTN_HEADS * ATTN_HEAD_DIM
CONV_DIM = 256
CONV_WIDTH = 3
POOL_WINDOWS = (2, 4, 8, 16)
POOL_DIM = 256
POOL_GD = POOL_DIM // len(POOL_WINDOWS)
IN_PROJ_WIDTH = 3 * ATTN_DIM + 3 * CONV_DIM + POOL_DIM
RMS_EPS = 1e-6

LANES = 128
FF_CHUNK = 256
N_FF_CHUNKS = D_FF // FF_CHUNK
FFN_ROWS = 1024
FFN_GROUP_ROWS = 512
PROJ_ROWS = 1024
PROJ_GROUP_ROWS = 512
HALO = 32
ATT_Q = 256
ATT_K = 256
ATT_TILES = 2
EARLY_ROWS = 192
UNDERFLOW_LOG = -105.0
VMEM_LIMIT = 56 * 1024 * 1024
FFN_VMEM_LIMIT = 60 * 1024 * 1024


def _rmsnorm(x, g):
    ms = jnp.mean(x * x, axis=-1, keepdims=True)
    return x * lax.rsqrt(ms + RMS_EPS) * g


def _ffn_kernel(x_ref, gpre_ref, gpost_ref, wg_ref, wu_ref, wd_ref, o_ref):
    groups = [slice(r, r + FFN_GROUP_ROWS) for r in range(0, FFN_ROWS, FFN_GROUP_ROWS)]
    hs = [_rmsnorm(x_ref[rows, :], gpre_ref[...]) for rows in groups]
    accs = [jnp.zeros((FFN_GROUP_ROWS, D_MODEL), F32) for _ in groups]
    for c in range(N_FF_CHUNKS):
        cols = slice(c * FF_CHUNK, (c + 1) * FF_CHUNK)
        for i, h in enumerate(hs):
            g = jnp.dot(h, wg_ref[:, cols], preferred_element_type=F32)
            u = jnp.dot(h, wu_ref[:, cols], preferred_element_type=F32)
            a = g * jax.nn.sigmoid(g) * u
            accs[i] = accs[i] + jnp.dot(a, wd_ref[cols, :], preferred_element_type=F32)
    for rows, acc in zip(groups, accs):
        o_ref[rows, :] = x_ref[rows, :] + 0.5 * _rmsnorm(acc, gpost_ref[...])


def _ffn(x2d, norm_g, w_gate, w_up, w_down, layer, half):
    n = x2d.shape[0]
    row_spec = pl.BlockSpec((FFN_ROWS, D_MODEL), lambda i: (i, 0))
    gain = lambda which: pl.BlockSpec((None, None, 1, D_MODEL), lambda i: (layer, which, 0, 0))
    resident = lambda rows, cols: pl.BlockSpec(
        (None, None, rows, cols), lambda i: (layer, half, 0, 0), pipeline_mode=pl.Buffered(1))
    pre, post = (0, 1) if half == 0 else (4, 5)
    return pl.pallas_call(
        _ffn_kernel,
        name="ffn",
        grid=(n // FFN_ROWS,),
        in_specs=[
            row_spec, gain(pre), gain(post),
            resident(D_MODEL, D_FF), resident(D_MODEL, D_FF), resident(D_FF, D_MODEL),
        ],
        out_specs=row_spec,
        out_shape=jax.ShapeDtypeStruct(x2d.shape, F32),
        compiler_params=pltpu.CompilerParams(
            dimension_semantics=("arbitrary",), vmem_limit_bytes=FFN_VMEM_LIMIT),
    )(x2d, norm_g, norm_g, w_gate, w_up, w_down)


def _in_proj_kernel(tiles_per_seq, x_ref, g_ref, w_ref, convw_ref, poolw_ref, pscale_ref,
                    q_ref, k_ref, v_ref, cp_ref, ubuf, pbuf, s2buf, s4buf, s8buf):
    tile_in_seq = pl.program_id(0) % tiles_per_seq

    @pl.when(tile_in_seq == 0)
    def _():
        ubuf[0:HALO, :] = jnp.zeros((HALO, CONV_DIM), F32)
        pbuf[0:HALO, :] = jnp.zeros((HALO, POOL_DIM), F32)

    @pl.when(tile_in_seq != 0)
    def _():
        ubuf[0:HALO, :] = ubuf[PROJ_ROWS:PROJ_ROWS + HALO, :]
        pbuf[0:HALO, :] = pbuf[PROJ_ROWS:PROJ_ROWS + HALO, :]

    starts = list(range(0, PROJ_ROWS, PROJ_GROUP_ROWS))
    mix_cols = slice(3 * ATTN_DIM, IN_PROJ_WIDTH)
    h = _rmsnorm(x_ref[0:PROJ_GROUP_ROWS, :], g_ref[...])
    for gi, r0 in enumerate(starts):
        out_rows = slice(r0, r0 + PROJ_GROUP_ROWS)
        mix_in = jnp.dot(h, w_ref[:, mix_cols], preferred_element_type=F32)
        q = jnp.dot(h, w_ref[:, 0:ATTN_DIM], preferred_element_type=F32)
        q_ref[out_rows, :] = (q * (1.0 / math.sqrt(ATTN_HEAD_DIM))).astype(BF16)
        _mix_rows(r0, tile_in_seq * PROJ_ROWS + r0, mix_in, convw_ref, poolw_ref, pscale_ref,
                  cp_ref, ubuf, pbuf, s2buf, s4buf, s8buf)
        k_ref[out_rows, :] = jnp.dot(h, w_ref[:, ATTN_DIM:2 * ATTN_DIM],
                                     preferred_element_type=F32).astype(BF16)
        h_next = None
        if gi + 1 < len(starts):
            nxt = starts[gi + 1]
            h_next = _rmsnorm(x_ref[nxt:nxt + PROJ_GROUP_ROWS, :], g_ref[...])
        v_ref[out_rows, :] = jnp.dot(h, w_ref[:, 2 * ATTN_DIM:3 * ATTN_DIM],
                                     preferred_element_type=F32).astype(BF16)
        h = h_next


def _mix_rows(r0, seq_pos, mix_in, convw_ref, poolw_ref, pscale_ref,
              cp_ref, ubuf, pbuf, s2buf, s4buf, s8buf):
    rows = PROJ_GROUP_ROWS
    out_rows = slice(r0, r0 + rows)
    gate_b = mix_in[:, 0:CONV_DIM]
    gate_c = mix_in[:, CONV_DIM:2 * CONV_DIM]
    conv_h = mix_in[:, 2 * CONV_DIM:3 * CONV_DIM]
    p = mix_in[:, 3 * CONV_DIM:]

    tok = r0 + HALO
    end = tok + rows
    ubuf[tok:end, :] = gate_c * conv_h
    cw = convw_ref[...]
    y = (cw[0:1, :] * ubuf[tok - 2:end - 2, :]
         + cw[1:2, :] * ubuf[tok - 1:end - 1, :]
         + cw[2:3, :] * ubuf[tok:end, :])
    cp_ref[out_rows, 0:CONV_DIM] = (gate_b * y).astype(BF16)

    pbuf[tok:end, :] = p
    s2buf[r0 + 8:end, :] = pbuf[r0 + 8:end, :] + pbuf[r0 + 7:end - 1, :]
    s4buf[r0 + 16:end, :] = s2buf[r0 + 16:end, :] + s2buf[r0 + 14:end - 2, :]
    s8buf[r0 + 24:end, :] = s4buf[r0 + 24:end, :] + s4buf[r0 + 20:end - 4, :]
    s16 = s8buf[tok:end, :] + s8buf[tok - 8:end - 8, :]
    group = lax.broadcasted_iota(jnp.int32, (rows, POOL_DIM), 1) // POOL_GD
    win_sum = jnp.where(group == 0, s2buf[tok:end, :],
                        jnp.where(group == 1, s4buf[tok:end, :],
                                  jnp.where(group == 2, s8buf[tok:end, :], s16)))
    window = jnp.where(group == 0, POOL_WINDOWS[0],
                       jnp.where(group == 1, POOL_WINDOWS[1],
                                 jnp.where(group == 2, POOL_WINDOWS[2], POOL_WINDOWS[3])))
    pos = seq_pos + lax.broadcasted_iota(jnp.int32, (rows, POOL_DIM), 0)
    count = jnp.minimum(pos + 1, window).astype(F32)
    d = win_sum / count - p
    pooled = jnp.dot(d, poolw_ref[...], preferred_element_type=F32)
    cp_ref[out_rows, CONV_DIM:] = (pooled * pscale_ref[...]).astype(BF16)


def _in_proj(x2d, seq, norm_g, w_in, conv_w, pool_w_bd, pool_scale, layer):
    n = x2d.shape[0]
    tiles_per_seq = seq // PROJ_ROWS
    row = lambda width: pl.BlockSpec((PROJ_ROWS, width), lambda i: (i, 0))
    of_layer = lambda rows, cols, **kw: pl.BlockSpec((None, rows, cols), lambda i: (layer, 0, 0), **kw)
    att = jax.ShapeDtypeStruct((n, ATTN_DIM), BF16)
    buf = lambda width: pltpu.VMEM((HALO + PROJ_ROWS, width), F32)
    return pl.pallas_call(
        functools.partial(_in_proj_kernel, tiles_per_seq),
        name="in_proj",
        grid=(n // PROJ_ROWS,),
        in_specs=[
            row(D_MODEL),
            pl.BlockSpec((None, None, 1, D_MODEL), lambda i: (layer, 2, 0, 0)),
            of_layer(D_MODEL, IN_PROJ_WIDTH, pipeline_mode=pl.Buffered(1)),
            of_layer(CONV_WIDTH, CONV_DIM), of_layer(POOL_DIM, POOL_DIM), of_layer(1, POOL_DIM),
        ],
        out_specs=[row(ATTN_DIM), row(ATTN_DIM), row(ATTN_DIM), row(CONV_DIM + POOL_DIM)],
        out_shape=[att, att, att, jax.ShapeDtypeStruct((n, CONV_DIM + POOL_DIM), BF16)],
        scratch_shapes=[buf(CONV_DIM), buf(POOL_DIM), buf(POOL_DIM), buf(POOL_DIM), buf(POOL_DIM)],
        compiler_params=pltpu.CompilerParams(
            dimension_semantics=("arbitrary",), vmem_limit_bytes=VMEM_LIMIT),
    )(x2d, norm_g, w_in, conv_w, pool_w_bd, pool_scale)


def _attn_scores(qm, k2):
    return lax.dot_general(qm, k2, (((1,), (1,)), ((), ())), preferred_element_type=F32)


def _attn_block(z, v2, carry, neg_tri, mask):
    neg_abs = lax.bitcast_convert_type(
        lax.bitcast_convert_type(z, jnp.uint32) | jnp.uint32(0x80000000), F32)
    soft = jnp.log(1.0 + jnp.exp(neg_abs))
    drop = jnp.maximum(z, 0.0) + soft
    log_beta = z - drop
    if mask is not None:
        drop = jnp.where(mask, drop, 0.0)
    between = jnp.dot(drop.astype(BF16), neg_tri, preferred_element_type=F32)
    a = jnp.exp(log_beta + between + carry)
    if mask is not None:
        a = jnp.where(mask, a, 0.0)
    out = jnp.dot(a.astype(BF16), v2, preferred_element_type=F32)
    new_carry = carry + between[:, 0:1] - drop[:, 0:1]
    return out, new_carry


def _attn_out_kernel(x_ref, q_ref, k_ref, v_ref, cp_ref, wout_ref, g_ref, o_ref,
                     qm_ref, mix_ref, acc_ref, carry_ref):
    for t in range(ATT_TILES):
        _attn_tile(pl.program_id(1) * ATT_TILES + t, slice(t * ATT_Q, (t + 1) * ATT_Q),
                   q_ref, k_ref, v_ref, cp_ref, qm_ref, mix_ref, acc_ref, carry_ref)
    m = jnp.dot(mix_ref[...], wout_ref[...], preferred_element_type=F32)
    o_ref[0] = x_ref[0] + _rmsnorm(m, g_ref[...])


def _attn_tile(qi, tile_rows, q_ref, k_ref, v_ref, cp_ref, qm_ref, mix_ref, acc_ref, carry_ref):
    n_pairs = ATTN_HEADS // 2
    early, late = slice(0, EARLY_ROWS), slice(EARLY_ROWS, ATT_Q)
    n_early = 2 * EARLY_ROWS
    row = lax.broadcasted_iota(jnp.int32, (ATT_Q, ATT_K), 0)
    col = lax.broadcasted_iota(jnp.int32, (ATT_Q, ATT_K), 1)
    neg_tri = jnp.where(row > col, -1.0, 0.0).astype(BF16)

    rows_early = lax.broadcasted_iota(jnp.int32, (EARLY_ROWS, ATT_K), 0)
    rows_late = EARLY_ROWS + lax.broadcasted_iota(jnp.int32, (ATT_Q - EARLY_ROWS, ATT_K), 0)
    tile_row = jnp.concatenate([rows_early, rows_early, rows_late, rows_late], axis=0)
    causal = lax.broadcasted_iota(jnp.int32, (2 * ATT_Q, ATT_K), 1) < tile_row
    first_head = lax.broadcasted_iota(jnp.int32, (ATT_Q, LANES), 1) < ATTN_HEAD_DIM
    pair_lanes = [slice(hp * LANES, (hp + 1) * LANES) for hp in range(n_pairs)]

    for hp in range(n_pairs):
        q2 = q_ref[0, tile_rows, pair_lanes[hp]]
        head0 = jnp.where(first_head, q2, jnp.zeros_like(q2))
        head1 = jnp.where(first_head, jnp.zeros_like(q2), q2)
        qm_ref[hp] = jnp.concatenate([head0[early], head1[early], head0[late], head1[late]], axis=0)

    def scores(hp, blocks_back, rows):
        start = pl.multiple_of((qi - blocks_back) * ATT_K, ATT_K)
        return _attn_scores(qm_ref[hp, rows, :], k_ref[0, pl.ds(start, ATT_K), pair_lanes[hp]])

    def sweep(items):
        z = scores(*items[0][:3])
        for n, (hp, blocks_back, rows, is_diagonal) in enumerate(items):
            z_next = scores(*items[n + 1][:3]) if n + 1 < len(items) else None
            start = pl.multiple_of((qi - blocks_back) * ATT_K, ATT_K)
            v2 = v_ref[0, pl.ds(start, ATT_K), pair_lanes[hp]]
            if is_diagonal:
                out, carry = _attn_block(z, v2, jnp.zeros((2 * ATT_Q, 1), F32), neg_tri, causal)
                acc_ref[hp, rows, :] = out
            else:
                out, carry = _attn_block(z, v2, carry_ref[hp, rows, :], neg_tri, None)
                acc_ref[hp, rows, :] += out
            carry_ref[hp, rows, :] = carry
            z = z_next

    everything, early_rows, late_rows = slice(0, 2 * ATT_Q), slice(0, n_early), slice(n_early, 2 * ATT_Q)

    def leading_blocks(with_previous):
        items = []
        for hp in range(n_pairs):
            items.append((hp, 0, everything, True))
            if with_previous:
                items.append((hp, 1, early_rows, False))
        sweep(items)

    pl.when(qi == 0)(lambda: leading_blocks(False))
    pl.when(qi > 0)(lambda: leading_blocks(True))

    def weight_left(rows):
        return jnp.max(carry_ref[:, rows, :]) > UNDERFLOW_LOG

    def accumulate(blocks_back, rows):
        sweep([(hp, blocks_back, rows, False) for hp in range(n_pairs)])

    late_left = weight_left(late_rows)
    early_left = weight_left(early_rows)
    pl.when(jnp.logical_and(qi > 0, late_left))(lambda: accumulate(1, late_rows))

    def body(state):
        blocks_back, _ = state
        accumulate(blocks_back, everything)
        return blocks_back + 1, weight_left(everything)

    lax.while_loop(lambda state: jnp.logical_and(state[0] <= qi, state[1]), body,
                   (jnp.int32(2), jnp.logical_or(early_left, late_left)))

    for hp in range(n_pairs):
        for rows, first, second in ((early, slice(0, EARLY_ROWS), slice(EARLY_ROWS, n_early)),
                                    (late, slice(n_early, n_early + ATT_Q - EARLY_ROWS),
                                     slice(n_early + ATT_Q - EARLY_ROWS, 2 * ATT_Q))):
            head0_lanes = lax.broadcasted_iota(jnp.int32, (rows.stop - rows.start, LANES), 1) < ATTN_HEAD_DIM
            out_rows = slice(tile_rows.start + rows.start, tile_rows.start + rows.stop)
            mix_ref[out_rows, pair_lanes[hp]] = jnp.where(
                head0_lanes, acc_ref[hp, first, :], acc_ref[hp, second, :])
    mix_ref[tile_rows, ATTN_DIM:] = cp_ref[0, tile_rows, :].astype(F32)


def _attn_out(x3d, q, k, v, cp, w_out, norm_g, layer):
    b, s, _ = x3d.shape
    tile = lambda width: pl.BlockSpec((1, ATT_TILES * ATT_Q, width), lambda bi, qi: (bi, qi, 0))
    seq = lambda width: pl.BlockSpec((1, s, width), lambda bi, qi: (bi, 0, 0))
    return pl.pallas_call(
        _attn_out_kernel,
        name="attn_out",
        grid=(b, s // (ATT_TILES * ATT_Q)),
        in_specs=[
            tile(D_MODEL), tile(ATTN_DIM), seq(ATTN_DIM), seq(ATTN_DIM), tile(CONV_DIM + POOL_DIM),
            pl.BlockSpec((None, D_MODEL, D_MODEL), lambda bi, qi: (layer, 0, 0)),
            pl.BlockSpec((None, None, 1, D_MODEL), lambda bi, qi: (layer, 3, 0, 0)),
        ],
        out_specs=tile(D_MODEL),
        out_shape=jax.ShapeDtypeStruct(x3d.shape, F32),
        scratch_shapes=[
            pltpu.VMEM((ATTN_HEADS // 2, 2 * ATT_Q, LANES), BF16),
            pltpu.VMEM((ATT_TILES * ATT_Q, D_MODEL), F32),
            pltpu.VMEM((ATTN_HEADS // 2, 2 * ATT_Q, LANES), F32),
            pltpu.VMEM((ATTN_HEADS // 2, 2 * ATT_Q, 1), F32),
        ],
        compiler_params=pltpu.CompilerParams(
            dimension_semantics=("arbitrary", "arbitrary"), vmem_limit_bytes=VMEM_LIMIT),
    )(x3d, q, k, v, cp, w_out, norm_g)


def _block_diag(pool_w):
    depth, groups, gd, _ = pool_w.shape
    eye = jnp.eye(groups, dtype=pool_w.dtype)
    return jnp.einsum("lgij,gh->lgihj", pool_w, eye).reshape(depth, groups * gd, groups * gd)


def kernel(x, norm_g, ffn_w_gate, ffn_w_up, ffn_w_down, w_in, conv_w, pool_w, pool_scale, w_out):
    b, s, d = x.shape
    assert d == D_MODEL and s % PROJ_ROWS == 0 and s % (ATT_TILES * ATT_Q) == 0 and (b * s) % FFN_ROWS == 0
    depth = norm_g.shape[0]
    gains = norm_g.reshape(depth, 6, 1, D_MODEL)
    pool_w_bd = _block_diag(pool_w)
    pool_scale = pool_scale.reshape(depth, 1, POOL_DIM)
    to3d = lambda t: t.reshape(b, s, t.shape[-1])
    x2d = x.reshape(b * s, d)
    for l in range(depth):
        x2d = _ffn(x2d, gains, ffn_w_gate, ffn_w_up, ffn_w_down, l, 0)
        q, k, v, cp = _in_proj(x2d, s, gains, w_in, conv_w, pool_w_bd, pool_scale, l)
        x2d = _attn_out(to3d(x2d), to3d(q), to3d(k), to3d(v), to3d(cp), w_out, gains, l).reshape(b * s, d)
        x2d = _ffn(x2d, gains, ffn_w_gate, ffn_w_up, ffn_w_down, l, 1)
    return x2d.reshape(b, s, d)
```

```python
import functools
import math

import jax
import jax.numpy as jnp
from jax import lax
from jax.experimental import pallas as pl
from jax.experimental.pallas import tpu as pltpu

F32 = jnp.float32
BF16 = jnp.bfloat16

D_MODEL = 1024
D_FF = 2816
ATTN_HEADS = 8
ATTN_HEAD_DIM = 64
ATTN_DIM = ATTN_HEADS * ATTN_HEAD_DIM
CONV_DIM = 256
CONV_WIDTH = 3
POOL_WINDOWS = (2, 4, 8, 16)
POOL_DIM = 256
POOL_GD = POOL_DIM // len(POOL_WINDOWS)
IN_PROJ_WIDTH = 3 * ATTN_DIM + 3 * CONV_DIM + POOL_DIM
RMS_EPS = 1e-6

LANES = 128
FF_CHUNK = 256
N_FF_CHUNKS = D_FF // FF_CHUNK
FFN_ROWS = 1024
FFN_GROUP_ROWS = 512
PROJ_ROWS = 1024
PROJ_GROUP_ROWS = 512
HALO = 32
ATT_Q = 256
ATT_K = 256
ATT_TILES = 4
EARLY_ROWS = 192
UNDERFLOW_LOG = -105.0
VMEM_LIMIT = 56 * 1024 * 1024
FFN_VMEM_LIMIT = 60 * 1024 * 1024


def _rmsnorm(x, g):
    ms = jnp.mean(x * x, axis=-1, keepdims=True)
    return x * lax.rsqrt(ms + RMS_EPS) * g


def _ffn_kernel(x_ref, gpre_ref, gpost_ref, wg_ref, wu_ref, wd_ref, o_ref):
    groups = [slice(r, r + FFN_GROUP_ROWS) for r in range(0, FFN_ROWS, FFN_GROUP_ROWS)]
    hs = [_rmsnorm(x_ref[rows, :], gpre_ref[...]) for rows in groups]
    accs = [jnp.zeros((FFN_GROUP_ROWS, D_MODEL), F32) for _ in groups]
    for c in range(N_FF_CHUNKS):
        cols = slice(c * FF_CHUNK, (c + 1) * FF_CHUNK)
        for i, h in enumerate(hs):
            g = jnp.dot(h, wg_ref[:, cols], preferred_element_type=F32)
            u = jnp.dot(h, wu_ref[:, cols], preferred_element_type=F32)
            a = g * jax.nn.sigmoid(g) * u
            accs[i] = accs[i] + jnp.dot(a, wd_ref[cols, :], preferred_element_type=F32)
    for rows, acc in zip(groups, accs):
        o_ref[rows, :] = x_ref[rows, :] + 0.5 * _rmsnorm(acc, gpost_ref[...])


def _ffn(x2d, norm_g, w_gate, w_up, w_down, layer, half):
    n = x2d.shape[0]
    row_spec = pl.BlockSpec((FFN_ROWS, D_MODEL), lambda i: (i, 0))
    gain = lambda which: pl.BlockSpec((None, None, 1, D_MODEL), lambda i: (layer, which, 0, 0))
    resident = lambda rows, cols: pl.BlockSpec(
        (None, None, rows, cols), lambda i: (layer, half, 0, 0), pipeline_mode=pl.Buffered(1))
    pre, post = (0, 1) if half == 0 else (4, 5)
    return pl.pallas_call(
        _ffn_kernel,
        name="ffn",
        grid=(n // FFN_ROWS,),
        in_specs=[
            row_spec, gain(pre), gain(post),
            resident(D_MODEL, D_FF), resident(D_MODEL, D_FF), resident(D_FF, D_MODEL),
        ],
        out_specs=row_spec,
        out_shape=jax.ShapeDtypeStruct(x2d.shape, F32),
        compiler_params=pltpu.CompilerParams(
            dimension_semantics=("arbitrary",), vmem_limit_bytes=FFN_VMEM_LIMIT),
    )(x2d, norm_g, norm_g, w_gate, w_up, w_down)


def _in_proj_kernel(tiles_per_seq, x_ref, g_ref, w_ref, convw_ref, poolw_ref, pscale_ref,
                    q_ref, k_ref, v_ref, cp_ref, ubuf, pbuf, s2buf, s4buf, s8buf):
    tile_in_seq = pl.program_id(0) % tiles_per_seq

    @pl.when(tile_in_seq == 0)
    def _():
        ubuf[0:HALO, :] = jnp.zeros((HALO, CONV_DIM), F32)
        pbuf[0:HALO, :] = jnp.zeros((HALO, POOL_DIM), F32)

    @pl.when(tile_in_seq != 0)
    def _():
        ubuf[0:HALO, :] = ubuf[PROJ_ROWS:PROJ_ROWS + HALO, :]
        pbuf[0:HALO, :] = pbuf[PROJ_ROWS:PROJ_ROWS + HALO, :]

    starts = list(range(0, PROJ_ROWS, PROJ_GROUP_ROWS))
    mix_cols = slice(3 * ATTN_DIM, IN_PROJ_WIDTH)
    h = _rmsnorm(x_ref[0:PROJ_GROUP_ROWS, :], g_ref[...])
    for gi, r0 in enumerate(starts):
        out_rows = slice(r0, r0 + PROJ_GROUP_ROWS)
        mix_in = jnp.dot(h, w_ref[:, mix_cols], preferred_element_type=F32)
        q = jnp.dot(h, w_ref[:, 0:ATTN_DIM], preferred_element_type=F32)
        q_ref[out_rows, :] = (q * (1.0 / math.sqrt(ATTN_HEAD_DIM))).astype(BF16)
        _mix_rows(r0, tile_in_seq * PROJ_ROWS + r0, mix_in, convw_ref, poolw_ref, pscale_ref,
                  cp_ref, ubuf, pbuf, s2buf, s4buf, s8buf)
        k_ref[out_rows, :] = jnp.dot(h, w_ref[:, ATTN_DIM:2 * ATTN_DIM],
                                     preferred_element_type=F32).astype(BF16)
        h_next = None
        if gi + 1 < len(starts):
            nxt = starts[gi + 1]
            h_next = _rmsnorm(x_ref[nxt:nxt + PROJ_GROUP_ROWS, :], g_ref[...])
        v_ref[out_rows, :] = jnp.dot(h, w_ref[:, 2 * ATTN_DIM:3 * ATTN_DIM],
                                     preferred_element_type=F32).astype(BF16)
        h = h_next


def _mix_rows(r0, seq_pos, mix_in, convw_ref, poolw_ref, pscale_ref,
              cp_ref, ubuf, pbuf, s2buf, s4buf, s8buf):
    rows = PROJ_GROUP_ROWS
    out_rows = slice(r0, r0 + rows)
    gate_b = mix_in[:, 0:CONV_DIM]
    gate_c = mix_in[:, CONV_DIM:2 * CONV_DIM]
    conv_h = mix_in[:, 2 * CONV_DIM:3 * CONV_DIM]
    p = mix_in[:, 3 * CONV_DIM:]

    tok = r0 + HALO
    end = tok + rows
    ubuf[tok:end, :] = gate_c * conv_h
    cw = convw_ref[...]
    y = (cw[0:1, :] * ubuf[tok - 2:end - 2, :]
         + cw[1:2, :] * ubuf[tok - 1:end - 1, :]
         + cw[2:3, :] * ubuf[tok:end, :])
    cp_ref[out_rows, 0:CONV_DIM] = (gate_b * y).astype(BF16)

    pbuf[tok:end, :] = p
    s2buf[r0 + 8:end, :] = pbuf[r0 + 8:end, :] + pbuf[r0 + 7:end - 1, :]
    s4buf[r0 + 16:end, :] = s2buf[r0 + 16:end, :] + s2buf[r0 + 14:end - 2, :]
    s8buf[r0 + 24:end, :] = s4buf[r0 + 24:end, :] + s4buf[r0 + 20:end - 4, :]
    s16 = s8buf[tok:end, :] + s8buf[tok - 8:end - 8, :]
    group = lax.broadcasted_iota(jnp.int32, (rows, POOL_DIM), 1) // POOL_GD
    win_sum = jnp.where(group == 0, s2buf[tok:end, :],
                        jnp.where(group == 1, s4buf[tok:end, :],
                                  jnp.where(group == 2, s8buf[tok:end, :], s16)))
    window = jnp.where(group == 0, POOL_WINDOWS[0],
                       jnp.where(group == 1, POOL_WINDOWS[1],
                                 jnp.where(group == 2, POOL_WINDOWS[2], POOL_WINDOWS[3])))
    pos = seq_pos + lax.broadcasted_iota(jnp.int32, (rows, POOL_DIM), 0)
    count = jnp.minimum(pos + 1, window).astype(F32)
    d = win_sum / count - p
    pooled = jnp.dot(d, poolw_ref[...], preferred_element_type=F32)
    cp_ref[out_rows, CONV_DIM:] = (pooled * pscale_ref[...]).astype(BF16)


def _in_proj(x2d, seq, norm_g, w_in, conv_w, pool_w_bd, pool_scale, layer):
    n = x2d.shape[0]
    tiles_per_seq = seq // PROJ_ROWS
    row = lambda width: pl.BlockSpec((PROJ_ROWS, width), lambda i: (i, 0))
    of_layer = lambda rows, cols, **kw: pl.BlockSpec((None, rows, cols), lambda i: (layer, 0, 0), **kw)
    att = jax.ShapeDtypeStruct((n, ATTN_DIM), BF16)
    buf = lambda width: pltpu.VMEM((HALO + PROJ_ROWS, width), F32)
    return pl.pallas_call(
        functools.partial(_in_proj_kernel, tiles_per_seq),
        name="in_proj",
        grid=(n // PROJ_ROWS,),
        in_specs=[
            row(D_MODEL),
            pl.BlockSpec((None, None, 1, D_MODEL), lambda i: (layer, 2, 0, 0)),
            of_layer(D_MODEL, IN_PROJ_WIDTH, pipeline_mode=pl.Buffered(1)),
            of_layer(CONV_WIDTH, CONV_DIM), of_layer(POOL_DIM, POOL_DIM), of_layer(1, POOL_DIM),
        ],
        out_specs=[row(ATTN_DIM), row(ATTN_DIM), row(ATTN_DIM), row(CONV_DIM + POOL_DIM)],
        out_shape=[att, att, att, jax.ShapeDtypeStruct((n, CONV_DIM + POOL_DIM), BF16)],
        scratch_shapes=[buf(CONV_DIM), buf(POOL_DIM), buf(POOL_DIM), buf(POOL_DIM), buf(POOL_DIM)],
        compiler_params=pltpu.CompilerParams(
            dimension_semantics=("arbitrary",), vmem_limit_bytes=VMEM_LIMIT),
    )(x2d, norm_g, w_in, conv_w, pool_w_bd, pool_scale)


def _attn_scores(qm, k2):
    return lax.dot_general(qm, k2, (((1,), (1,)), ((), ())), preferred_element_type=F32)


def _attn_block(z, v2, carry, neg_tri, mask):
    neg_abs = lax.bitcast_convert_type(
        lax.bitcast_convert_type(z, jnp.uint32) | jnp.uint32(0x80000000), F32)
    soft = jnp.log(1.0 + jnp.exp(neg_abs))
    drop = jnp.maximum(z, 0.0) + soft
    log_beta = z - drop
    if mask is not None:
        drop = jnp.where(mask, drop, 0.0)
    between = jnp.dot(drop.astype(BF16), neg_tri, preferred_element_type=F32)
    a = jnp.exp(log_beta + between + carry)
    if mask is not None:
        a = jnp.where(mask, a, 0.0)
    out = jnp.dot(a.astype(BF16), v2, preferred_element_type=F32)
    new_carry = carry + between[:, 0:1] - drop[:, 0:1]
    return out, new_carry


def _attn_out_kernel(x_ref, q_ref, k_ref, v_ref, cp_ref, wout_ref, g_ref, o_ref,
                     qm_ref, mix_ref, acc_ref, carry_ref):
    for t in range(ATT_TILES):
        _attn_tile(pl.program_id(1) * ATT_TILES + t, slice(t * ATT_Q, (t + 1) * ATT_Q),
                   q_ref, k_ref, v_ref, cp_ref, qm_ref, mix_ref, acc_ref, carry_ref)
    m = jnp.dot(mix_ref[...], wout_ref[...], preferred_element_type=F32)
    o_ref[0] = x_ref[0] + _rmsnorm(m, g_ref[...])


def _attn_tile(qi, tile_rows, q_ref, k_ref, v_ref, cp_ref, qm_ref, mix_ref, acc_ref, carry_ref):
    n_pairs = ATTN_HEADS // 2
    early, late = slice(0, EARLY_ROWS), slice(EARLY_ROWS, ATT_Q)
    n_early = 2 * EARLY_ROWS
    row = lax.broadcasted_iota(jnp.int32, (ATT_Q, ATT_K), 0)
    col = lax.broadcasted_iota(jnp.int32, (ATT_Q, ATT_K), 1)
    neg_tri = jnp.where(row > col, -1.0, 0.0).astype(BF16)

    rows_early = lax.broadcasted_iota(jnp.int32, (EARLY_ROWS, ATT_K), 0)
    rows_late = EARLY_ROWS + lax.broadcasted_iota(jnp.int32, (ATT_Q - EARLY_ROWS, ATT_K), 0)
    tile_row = jnp.concatenate([rows_early, rows_early, rows_late, rows_late], axis=0)
    causal = lax.broadcasted_iota(jnp.int32, (2 * ATT_Q, ATT_K), 1) < tile_row
    first_head = lax.broadcasted_iota(jnp.int32, (ATT_Q, LANES), 1) < ATTN_HEAD_DIM
    pair_lanes = [slice(hp * LANES, (hp + 1) * LANES) for hp in range(n_pairs)]

    for hp in range(n_pairs):
        q2 = q_ref[0, tile_rows, pair_lanes[hp]]
        head0 = jnp.where(first_head, q2, jnp.zeros_like(q2))
        head1 = jnp.where(first_head, jnp.zeros_like(q2), q2)
        qm_ref[hp] = jnp.concatenate([head0[early], head1[early], head0[late], head1[late]], axis=0)

    def scores(hp, blocks_back, rows):
        start = pl.multiple_of((qi - blocks_back) * ATT_K, ATT_K)
        return _attn_scores(qm_ref[hp, rows, :], k_ref[0, pl.ds(start, ATT_K), pair_lanes[hp]])

    def sweep(items):
        z = scores(*items[0][:3])
        for n, (hp, blocks_back, rows, is_diagonal) in enumerate(items):
            z_next = scores(*items[n + 1][:3]) if n + 1 < len(items) else None
            start = pl.multiple_of((qi - blocks_back) * ATT_K, ATT_K)
            v2 = v_ref[0, pl.ds(start, ATT_K), pair_lanes[hp]]
            if is_diagonal:
                out, carry = _attn_block(z, v2, jnp.zeros((2 * ATT_Q, 1), F32), neg_tri, causal)
                acc_ref[hp, rows, :] = out
            else:
                out, carry = _attn_block(z, v2, carry_ref[hp, rows, :], neg_tri, None)
                acc_ref[hp, rows, :] += out
            carry_ref[hp, rows, :] = carry
            z = z_next

    everything, early_rows, late_rows = slice(0, 2 * ATT_Q), slice(0, n_early), slice(n_early, 2 * ATT_Q)

    def leading_blocks(with_previous):
        items = []
        for hp in range(n_pairs):
            items.append((hp, 0, everything, True))
            if with_previous:
                items.append((hp, 1, early_rows, False))
        sweep(items)

    pl.when(qi == 0)(lambda: leading_blocks(False))
    pl.when(qi > 0)(lambda: leading_blocks(True))

    def weight_left(rows):
        return jnp.max(carry_ref[:, rows, :]) > UNDERFLOW_LOG

    def accumulate(blocks_back, rows):
        sweep([(hp, blocks_back, rows, False) for hp in range(n_pairs)])

    late_left = weight_left(late_rows)
    early_left = weight_left(early_rows)
    pl.when(jnp.logical_and(qi > 0, late_left))(lambda: accumulate(1, late_rows))

    def body(state):
        blocks_back, _ = state
        accumulate(blocks_back, everything)
        return blocks_back + 1, weight_left(everything)

    lax.while_loop(lambda state: jnp.logical_and(state[0] <= qi, state[1]), body,
                   (jnp.int32(2), jnp.logical_or(early_left, late_left)))

    for hp in range(n_pairs):
        for rows, first, second in ((early, slice(0, EARLY_ROWS), slice(EARLY_ROWS, n_early)),
                                    (late, slice(n_early, n_early + ATT_Q - EARLY_ROWS),
                                     slice(n_early + ATT_Q - EARLY_ROWS, 2 * ATT_Q))):
            head0_lanes = lax.broadcasted_iota(jnp.int32, (rows.stop - rows.start, LANES), 1) < ATTN_HEAD_DIM
            out_rows = slice(tile_rows.start + rows.start, tile_rows.start + rows.stop)
            mix_ref[out_rows, pair_lanes[hp]] = jnp.where(
                head0_lanes, acc_ref[hp, first, :], acc_ref[hp, second, :])
    mix_ref[tile_rows, ATTN_DIM:] = cp_ref[0, tile_rows, :].astype(F32)


def _attn_out(x3d, q, k, v, cp, w_out, norm_g, layer):
    b, s, _ = x3d.shape
    tile = lambda width: pl.BlockSpec((1, ATT_TILES * ATT_Q, width), lambda bi, qi: (bi, qi, 0))
    seq = lambda width: pl.BlockSpec((1, s, width), lambda bi, qi: (bi, 0, 0))
    return pl.pallas_call(
        _attn_out_kernel,
        name="attn_out",
        grid=(b, s // (ATT_TILES * ATT_Q)),
        in_specs=[
            tile(D_MODEL), tile(ATTN_DIM), seq(ATTN_DIM), seq(ATTN_DIM), tile(CONV_DIM + POOL_DIM),
            pl.BlockSpec((None, D_MODEL, D_MODEL), lambda bi, qi: (layer, 0, 0)),
            pl.BlockSpec((None, None, 1, D_MODEL), lambda bi, qi: (layer, 3, 0, 0)),
        ],
        out_specs=tile(D_MODEL),
        out_shape=jax.ShapeDtypeStruct(x3d.shape, F32),
        scratch_shapes=[
            pltpu.VMEM((ATTN_HEADS // 2, 2 * ATT_Q, LANES), BF16),
            pltpu.VMEM((ATT_TILES * ATT_Q, D_MODEL), F32),
            pltpu.VMEM((ATTN_HEADS // 2, 2 * ATT_Q, LANES), F32),
            pltpu.VMEM((ATTN_HEADS // 2, 2 * ATT_Q, 1), F32),
        ],
        compiler_params=pltpu.CompilerParams(
            dimension_semantics=("arbitrary", "arbitrary"), vmem_limit_bytes=VMEM_LIMIT),
    )(x3d, q, k, v, cp, w_out, norm_g)


def _block_diag(pool_w):
    depth, groups, gd, _ = pool_w.shape
    eye = jnp.eye(groups, dtype=pool_w.dtype)
    return jnp.einsum("lgij,gh->lgihj", pool_w, eye).reshape(depth, groups * gd, groups * gd)


def kernel(x, norm_g, ffn_w_gate, ffn_w_up, ffn_w_down, w_in, conv_w, pool_w, pool_scale, w_out):
    b, s, d = x.shape
    assert d == D_MODEL and s % PROJ_ROWS == 0 and s % (ATT_TILES * ATT_Q) == 0 and (b * s) % FFN_ROWS == 0
    depth = norm_g.shape[0]
    gains = norm_g.reshape(depth, 6, 1, D_MODEL)
    pool_w_bd = _block_diag(pool_w)
    pool_scale = pool_scale.reshape(depth, 1, POOL_DIM)
    to3d = lambda t: t.reshape(b, s, t.shape[-1])
    x2d = x.reshape(b * s, d)
    for l in range(depth):
        x2d = _ffn(x2d, gains, ffn_w_gate, ffn_w_up, ffn_w_down, l, 0)
        q, k, v, cp = _in_proj(x2d, s, gains, w_in, conv_w, pool_w_bd, pool_scale, l)
        x2d = _attn_out(to3d(x2d), to3d(q), to3d(k), to3d(v), to3d(cp), w_out, gains, l).reshape(b * s, d)
        x2d = _ffn(x2d, gains, ffn_w_gate, ffn_w_up, ffn_w_down, l, 1)
    return x2d.reshape(b, s, d)
```

```python
import functools
import math

import jax
import jax.numpy as jnp
from jax import lax
from jax.experimental import pallas as pl
from jax.experimental.pallas import tpu as pltpu

F32 = jnp.float32
BF16 = jnp.bfloat16

D_MODEL = 1024
D_FF = 2816
ATTN_HEADS = 8
ATTN_HEAD_DIM = 64
ATTN_DIM = ATTN_HEADS * ATTN_HEAD_DIM
CONV_DIM = 256
CONV_WIDTH = 3
POOL_WINDOWS = (2, 4, 8, 16)
POOL_DIM = 256
POOL_GD = POOL_DIM // len(POOL_WINDOWS)
IN_PROJ_WIDTH = 3 * ATTN_DIM + 3 * CONV_DIM + POOL_DIM
RMS_EPS = 1e-6

LANES = 128
FF_CHUNK = 256
N_FF_CHUNKS = D_FF // FF_CHUNK
FFN_ROWS = 1024
FFN_GROUP_ROWS = 512
PROJ_ROWS = 1024
PROJ_GROUP_ROWS = 512
HALO = 32
ATT_Q = 256
ATT_K = 256
ATT_TILES = 4
EARLY_ROWS = 192
UNDERFLOW_LOG = -105.0
VMEM_LIMIT = 56 * 1024 * 1024
FFN_VMEM_LIMIT = 60 * 1024 * 1024


def _rmsnorm(x, g):
    ms = jnp.mean(x * x, axis=-1, keepdims=True)
    return x * lax.rsqrt(ms + RMS_EPS) * g


def _ffn_kernel(x_ref, gpre_ref, gpost_ref, wg_ref, wu_ref, wd_ref, o_ref):
    groups = [slice(r, r + FFN_GROUP_ROWS) for r in range(0, FFN_ROWS, FFN_GROUP_ROWS)]
    hs = [_rmsnorm(x_ref[rows, :], gpre_ref[...]) for rows in groups]
    accs = [jnp.zeros((FFN_GROUP_ROWS, D_MODEL), F32) for _ in groups]
    for c in range(N_FF_CHUNKS):
        cols = slice(c * FF_CHUNK, (c + 1) * FF_CHUNK)
        for i, h in enumerate(hs):
            g = jnp.dot(h, wg_ref[:, cols], preferred_element_type=F32)
            u = jnp.dot(h, wu_ref[:, cols], preferred_element_type=F32)
            a = g * jax.nn.sigmoid(g) * u
            accs[i] = accs[i] + jnp.dot(a, wd_ref[cols, :], preferred_element_type=F32)
    for rows, acc in zip(groups, accs):
        o_ref[rows, :] = x_ref[rows, :] + 0.5 * _rmsnorm(acc, gpost_ref[...])


def _ffn(x2d, norm_g, w_gate, w_up, w_down, layer, half):
    n = x2d.shape[0]
    row_spec = pl.BlockSpec((FFN_ROWS, D_MODEL), lambda i: (i, 0))
    gain = lambda which: pl.BlockSpec((None, None, 1, D_MODEL), lambda i: (layer, which, 0, 0))
    resident = lambda rows, cols: pl.BlockSpec(
        (None, None, rows, cols), lambda i: (layer, half, 0, 0), pipeline_mode=pl.Buffered(1))
    pre, post = (0, 1) if half == 0 else (4, 5)
    return pl.pallas_call(
        _ffn_kernel,
        name="ffn",
        grid=(n // FFN_ROWS,),
        in_specs=[
            row_spec, gain(pre), gain(post),
            resident(D_MODEL, D_FF), resident(D_MODEL, D_FF), resident(D_FF, D_MODEL),
        ],
        out_specs=row_spec,
        out_shape=jax.ShapeDtypeStruct(x2d.shape, F32),
        compiler_params=pltpu.CompilerParams(
            dimension_semantics=("arbitrary",), vmem_limit_bytes=FFN_VMEM_LIMIT),
    )(x2d, norm_g, norm_g, w_gate, w_up, w_down)


def _in_proj_kernel(tiles_per_seq, x_ref, g_ref, w_ref, convw_ref, poolw_ref, pscale_ref,
                    q_ref, k_ref, v_ref, cp_ref, ubuf, pbuf, s2buf, s4buf, s8buf):
    tile_in_seq = pl.program_id(0) % tiles_per_seq

    @pl.when(tile_in_seq == 0)
    def _():
        ubuf[0:HALO, :] = jnp.zeros((HALO, CONV_DIM), F32)
        pbuf[0:HALO, :] = jnp.zeros((HALO, POOL_DIM), F32)

    @pl.when(tile_in_seq != 0)
    def _():
        ubuf[0:HALO, :] = ubuf[PROJ_ROWS:PROJ_ROWS + HALO, :]
        pbuf[0:HALO, :] = pbuf[PROJ_ROWS:PROJ_ROWS + HALO, :]

    starts = list(range(0, PROJ_ROWS, PROJ_GROUP_ROWS))
    mix_cols = slice(3 * ATTN_DIM, IN_PROJ_WIDTH)
    h = _rmsnorm(x_ref[0:PROJ_GROUP_ROWS, :], g_ref[...])
    for gi, r0 in enumerate(starts):
        out_rows = slice(r0, r0 + PROJ_GROUP_ROWS)
        mix_in = jnp.dot(h, w_ref[:, mix_cols], preferred_element_type=F32)
        q = jnp.dot(h, w_ref[:, 0:ATTN_DIM], preferred_element_type=F32)
        q_ref[out_rows, :] = (q * (1.0 / math.sqrt(ATTN_HEAD_DIM))).astype(BF16)
        _mix_rows(r0, tile_in_seq * PROJ_ROWS + r0, mix_in, convw_ref, poolw_ref, pscale_ref,
                  cp_ref, ubuf, pbuf, s2buf, s4buf, s8buf)
        k_ref[out_rows, :] = jnp.dot(h, w_ref[:, ATTN_DIM:2 * ATTN_DIM],
                                     preferred_element_type=F32).astype(BF16)
        h_next = None
        if gi + 1 < len(starts):
            nxt = starts[gi + 1]
            h_next = _rmsnorm(x_ref[nxt:nxt + PROJ_GROUP_ROWS, :], g_ref[...])
        v_ref[out_rows, :] = jnp.dot(h, w_ref[:, 2 * ATTN_DIM:3 * ATTN_DIM],
                                     preferred_element_type=F32).astype(BF16)
        h = h_next


def _mix_rows(r0, seq_pos, mix_in, convw_ref, poolw_ref, pscale_ref,
              cp_ref, ubuf, pbuf, s2buf, s4buf, s8buf):
    rows = PROJ_GROUP_ROWS
    out_rows = slice(r0, r0 + rows)
    gate_b = mix_in[:, 0:CONV_DIM]
    gate_c = mix_in[:, CONV_DIM:2 * CONV_DIM]
    conv_h = mix_in[:, 2 * CONV_DIM:3 * CONV_DIM]
    p = mix_in[:, 3 * CONV_DIM:]

    tok = r0 + HALO
    end = tok + rows
    ubuf[tok:end, :] = gate_c * conv_h
    cw = convw_ref[...]
    y = (cw[0:1, :] * ubuf[tok - 2:end - 2, :]
         + cw[1:2, :] * ubuf[tok - 1:end - 1, :]
         + cw[2:3, :] * ubuf[tok:end, :])
    cp_ref[out_rows, 0:CONV_DIM] = (gate_b * y).astype(BF16)

    pbuf[tok:end, :] = p
    s2buf[r0 + 8:end, :] = pbuf[r0 + 8:end, :] + pbuf[r0 + 7:end - 1, :]
    s4buf[r0 + 16:end, :] = s2buf[r0 + 16:end, :] + s2buf[r0 + 14:end - 2, :]
    s8buf[r0 + 24:end, :] = s4buf[r0 + 24:end, :] + s4buf[r0 + 20:end - 4, :]
    s16 = s8buf[tok:end, :] + s8buf[tok - 8:end - 8, :]
    group = lax.broadcasted_iota(jnp.int32, (rows, POOL_DIM), 1) // POOL_GD
    win_sum = jnp.where(group == 0, s2buf[tok:end, :],
                        jnp.where(group == 1, s4buf[tok:end, :],
                                  jnp.where(group == 2, s8buf[tok:end, :], s16)))
    window = jnp.where(group == 0, POOL_WINDOWS[0],
                       jnp.where(group == 1, POOL_WINDOWS[1],
                                 jnp.where(group == 2, POOL_WINDOWS[2], POOL_WINDOWS[3])))
    pos = seq_pos + lax.broadcasted_iota(jnp.int32, (rows, POOL_DIM), 0)
    count = jnp.minimum(pos + 1, window).astype(F32)
    d = win_sum / count - p
    pooled = jnp.dot(d, poolw_ref[...], preferred_element_type=F32)
    cp_ref[out_rows, CONV_DIM:] = (pooled * pscale_ref[...]).astype(BF16)


def _in_proj(x2d, seq, norm_g, w_in, conv_w, pool_w_bd, pool_scale, layer):
    n = x2d.shape[0]
    tiles_per_seq = seq // PROJ_ROWS
    row = lambda width: pl.BlockSpec((PROJ_ROWS, width), lambda i: (i, 0))
    of_layer = lambda rows, cols, **kw: pl.BlockSpec((None, rows, cols), lambda i: (layer, 0, 0), **kw)
    att = jax.ShapeDtypeStruct((n, ATTN_DIM), BF16)
    buf = lambda width: pltpu.VMEM((HALO + PROJ_ROWS, width), F32)
    return pl.pallas_call(
        functools.partial(_in_proj_kernel, tiles_per_seq),
        name="in_proj",
        grid=(n // PROJ_ROWS,),
        in_specs=[
            row(D_MODEL),
            pl.BlockSpec((None, None, 1, D_MODEL), lambda i: (layer, 2, 0, 0)),
            of_layer(D_MODEL, IN_PROJ_WIDTH, pipeline_mode=pl.Buffered(1)),
            of_layer(CONV_WIDTH, CONV_DIM), of_layer(POOL_DIM, POOL_DIM), of_layer(1, POOL_DIM),
        ],
        out_specs=[row(ATTN_DIM), row(ATTN_DIM), row(ATTN_DIM), row(CONV_DIM + POOL_DIM)],
        out_shape=[att, att, att, jax.ShapeDtypeStruct((n, CONV_DIM + POOL_DIM), BF16)],
        scratch_shapes=[buf(CONV_DIM), buf(POOL_DIM), buf(POOL_DIM), buf(POOL_DIM), buf(POOL_DIM)],
        compiler_params=pltpu.CompilerParams(
            dimension_semantics=("arbitrary",), vmem_limit_bytes=VMEM_LIMIT),
    )(x2d, norm_g, w_in, conv_w, pool_w_bd, pool_scale)


def _attn_scores(qm, k2):
    return lax.dot_general(qm, k2, (((1,), (1,)), ((), ())), preferred_element_type=F32)


def _attn_block(z, v2, carry, neg_tri, mask):
    neg_abs = lax.bitcast_convert_type(
        lax.bitcast_convert_type(z, jnp.uint32) | jnp.uint32(0x80000000), F32)
    soft = jnp.log(1.0 + jnp.exp(neg_abs))
    drop = jnp.maximum(z, 0.0) + soft
    log_beta = z - drop
    if mask is not None:
        drop = jnp.where(mask, drop, 0.0)
    between = jnp.dot(drop.astype(BF16), neg_tri, preferred_element_type=F32)
    a = jnp.exp(log_beta + between + carry)
    if mask is not None:
        a = jnp.where(mask, a, 0.0)
    out = jnp.dot(a.astype(BF16), v2, preferred_element_type=F32)
    new_carry = carry + between[:, 0:1] - drop[:, 0:1]
    return out, new_carry


def _attn_out_kernel(x_ref, q_ref, k_ref, v_ref, cp_ref, wout_ref, g_ref, o_ref,
                     qm_ref, mix_ref, acc_ref, carry_ref):
    def tile(t, _):
        _attn_tile(pl.program_id(1) * ATT_TILES + t, t * ATT_Q,
                   q_ref, k_ref, v_ref, cp_ref, qm_ref, mix_ref, acc_ref, carry_ref)
        return 0

    lax.fori_loop(0, ATT_TILES, tile, 0)
    m = jnp.dot(mix_ref[...], wout_ref[...], preferred_element_type=F32)
    o_ref[0] = x_ref[0] + _rmsnorm(m, g_ref[...])


def _attn_tile(qi, row0, q_ref, k_ref, v_ref, cp_ref, qm_ref, mix_ref, acc_ref, carry_ref):
    n_pairs = ATTN_HEADS // 2
    early, late = slice(0, EARLY_ROWS), slice(EARLY_ROWS, ATT_Q)
    n_early = 2 * EARLY_ROWS
    row = lax.broadcasted_iota(jnp.int32, (ATT_Q, ATT_K), 0)
    col = lax.broadcasted_iota(jnp.int32, (ATT_Q, ATT_K), 1)
    neg_tri = jnp.where(row > col, -1.0, 0.0).astype(BF16)

    rows_early = lax.broadcasted_iota(jnp.int32, (EARLY_ROWS, ATT_K), 0)
    rows_late = EARLY_ROWS + lax.broadcasted_iota(jnp.int32, (ATT_Q - EARLY_ROWS, ATT_K), 0)
    tile_row = jnp.concatenate([rows_early, rows_early, rows_late, rows_late], axis=0)
    causal = lax.broadcasted_iota(jnp.int32, (2 * ATT_Q, ATT_K), 1) < tile_row
    first_head = lax.broadcasted_iota(jnp.int32, (ATT_Q, LANES), 1) < ATTN_HEAD_DIM
    pair_lanes = [slice(hp * LANES, (hp + 1) * LANES) for hp in range(n_pairs)]

    for hp in range(n_pairs):
        q2 = q_ref[0, pl.ds(pl.multiple_of(row0, ATT_Q), ATT_Q), pair_lanes[hp]]
        head0 = jnp.where(first_head, q2, jnp.zeros_like(q2))
        head1 = jnp.where(first_head, jnp.zeros_like(q2), q2)
        qm_ref[hp] = jnp.concatenate([head0[early], head1[early], head0[late], head1[late]], axis=0)

    def scores(hp, blocks_back, rows):
        start = pl.multiple_of((qi - blocks_back) * ATT_K, ATT_K)
        return _attn_scores(qm_ref[hp, rows, :], k_ref[0, pl.ds(start, ATT_K), pair_lanes[hp]])

    def sweep(items):
        z = scores(*items[0][:3])
        for n, (hp, blocks_back, rows, is_diagonal) in enumerate(items):
            z_next = scores(*items[n + 1][:3]) if n + 1 < len(items) else None
            start = pl.multiple_of((qi - blocks_back) * ATT_K, ATT_K)
            v2 = v_ref[0, pl.ds(start, ATT_K), pair_lanes[hp]]
            if is_diagonal:
                out, carry = _attn_block(z, v2, jnp.zeros((2 * ATT_Q, 1), F32), neg_tri, causal)
                acc_ref[hp, rows, :] = out
            else:
                out, carry = _attn_block(z, v2, carry_ref[hp, rows, :], neg_tri, None)
                acc_ref[hp, rows, :] += out
            carry_ref[hp, rows, :] = carry
            z = z_next

    everything, early_rows, late_rows = slice(0, 2 * ATT_Q), slice(0, n_early), slice(n_early, 2 * ATT_Q)

    def leading_blocks(with_previous):
        items = []
        for hp in range(n_pairs):
            items.append((hp, 0, everything, True))
            if with_previous:
                items.append((hp, 1, early_rows, False))
        sweep(items)

    pl.when(qi == 0)(lambda: leading_blocks(False))
    pl.when(qi > 0)(lambda: leading_blocks(True))

    def weight_left(rows):
        return jnp.max(carry_ref[:, rows, :]) > UNDERFLOW_LOG

    def accumulate(blocks_back, rows):
        sweep([(hp, blocks_back, rows, False) for hp in range(n_pairs)])

    late_left = weight_left(late_rows)
    early_left = weight_left(early_rows)
    pl.when(jnp.logical_and(qi > 0, late_left))(lambda: accumulate(1, late_rows))

    def body(state):
        blocks_back, _ = state
        accumulate(blocks_back, everything)
        return blocks_back + 1, weight_left(everything)

    lax.while_loop(lambda state: jnp.logical_and(state[0] <= qi, state[1]), body,
                   (jnp.int32(2), jnp.logical_or(early_left, late_left)))

    for hp in range(n_pairs):
        for rows, first, second in ((early, slice(0, EARLY_ROWS), slice(EARLY_ROWS, n_early)),
                                    (late, slice(n_early, n_early + ATT_Q - EARLY_ROWS),
                                     slice(n_early + ATT_Q - EARLY_ROWS, 2 * ATT_Q))):
            head0_lanes = lax.broadcasted_iota(jnp.int32, (rows.stop - rows.start, LANES), 1) < ATTN_HEAD_DIM
            out_rows = pl.ds(pl.multiple_of(row0 + rows.start, ATT_Q - EARLY_ROWS), rows.stop - rows.start)
            mix_ref[out_rows, pair_lanes[hp]] = jnp.where(
                head0_lanes, acc_ref[hp, first, :], acc_ref[hp, second, :])
    tile_rows = pl.ds(pl.multiple_of(row0, ATT_Q), ATT_Q)
    mix_ref[tile_rows, ATTN_DIM:] = cp_ref[0, tile_rows, :].astype(F32)


def _attn_out(x3d, q, k, v, cp, w_out, norm_g, layer):
    b, s, _ = x3d.shape
    tile = lambda width: pl.BlockSpec((1, ATT_TILES * ATT_Q, width), lambda bi, qi: (bi, qi, 0))
    seq = lambda width: pl.BlockSpec((1, s, width), lambda bi, qi: (bi, 0, 0))
    return pl.pallas_call(
        _attn_out_kernel,
        name="attn_out",
        grid=(b, s // (ATT_TILES * ATT_Q)),
        in_specs=[
            tile(D_MODEL), tile(ATTN_DIM), seq(ATTN_DIM), seq(ATTN_DIM), tile(CONV_DIM + POOL_DIM),
            pl.BlockSpec((None, D_MODEL, D_MODEL), lambda bi, qi: (layer, 0, 0)),
            pl.BlockSpec((None, None, 1, D_MODEL), lambda bi, qi: (layer, 3, 0, 0)),
        ],
        out_specs=tile(D_MODEL),
        out_shape=jax.ShapeDtypeStruct(x3d.shape, F32),
        scratch_shapes=[
            pltpu.VMEM((ATTN_HEADS // 2, 2 * ATT_Q, LANES), BF16),
            pltpu.VMEM((ATT_TILES * ATT_Q, D_MODEL), F32),
            pltpu.VMEM((ATTN_HEADS // 2, 2 * ATT_Q, LANES), F32),
            pltpu.VMEM((ATTN_HEADS // 2, 2 * ATT_Q, 1), F32),
        ],
        compiler_params=pltpu.CompilerParams(
            dimension_semantics=("arbitrary", "arbitrary"), vmem_limit_bytes=VMEM_LIMIT),
    )(x3d, q, k, v, cp, w_out, norm_g)


def _block_diag(pool_w):
    depth, groups, gd, _ = pool_w.shape
    eye = jnp.eye(groups, dtype=pool_w.dtype)
    return jnp.einsum("lgij,gh->lgihj", pool_w, eye).reshape(depth, groups * gd, groups * gd)


def kernel(x, norm_g, ffn_w_gate, ffn_w_up, ffn_w_down, w_in, conv_w, pool_w, pool_scale, w_out):
    b, s, d = x.shape
    assert d == D_MODEL and s % PROJ_ROWS == 0 and s % (ATT_TILES * ATT_Q) == 0 and (b * s) % FFN_ROWS == 0
    depth = norm_g.shape[0]
    gains = norm_g.reshape(depth, 6, 1, D_MODEL)
    pool_w_bd = _block_diag(pool_w)
    pool_scale = pool_scale.reshape(depth, 1, POOL_DIM)
    to3d = lambda t: t.reshape(b, s, t.shape[-1])
    x2d = x.reshape(b * s, d)
    for l in range(depth):
        x2d = _ffn(x2d, gains, ffn_w_gate, ffn_w_up, ffn_w_down, l, 0)
        q, k, v, cp = _in_proj(x2d, s, gains, w_in, conv_w, pool_w_bd, pool_scale, l)
        x2d = _attn_out(to3d(x2d), to3d(q), to3d(k), to3d(v), to3d(cp), w_out, gains, l).reshape(b * s, d)
        x2d = _ffn(x2d, gains, ffn_w_gate, ffn_w_up, ffn_w_down, l, 1)
    return x2d.reshape(b, s, d)
```

```python
import functools
import math

import jax
import jax.numpy as jnp
from jax import lax
from jax.experimental import pallas as pl
from jax.experimental.pallas import tpu as pltpu

F32 = jnp.float32
BF16 = jnp.bfloat16

D_MODEL = 1024
D_FF = 2816
ATTN_HEADS = 8
ATTN_HEAD_DIM = 64
ATTN_DIM = ATTN_HEADS * ATTN_HEAD_DIM
CONV_DIM = 256
CONV_WIDTH = 3
POOL_WINDOWS = (2, 4, 8, 16)
POOL_DIM = 256
POOL_GD = POOL_DIM // len(POOL_WINDOWS)
IN_PROJ_WIDTH = 3 * ATTN_DIM + 3 * CONV_DIM + POOL_DIM
RMS_EPS = 1e-6

LANES = 128
FF_CHUNK = 256
N_FF_CHUNKS = D_FF // FF_CHUNK
FFN_ROWS = 1024
FFN_GROUP_ROWS = 512
PROJ_ROWS = 1024
PROJ_GROUP_ROWS = 512
HALO = 32
ATT_Q = 256
ATT_K = 256
ATT_TILES = 4
EARLY_ROWS = 192
QK_AHEAD = 2
UNDERFLOW_LOG = -105.0
V7X_VMEM_BYTES = 64 * 1024 * 1024
VMEM_LIMIT = V7X_VMEM_BYTES * 7 // 8
FFN_VMEM_LIMIT = V7X_VMEM_BYTES * 15 // 16


def _rmsnorm(x, g):
    ms = jnp.mean(x * x, axis=-1, keepdims=True)
    return x * lax.rsqrt(ms + RMS_EPS) * g


def _ffn_kernel(x_ref, gpre_ref, gpost_ref, wg_ref, wu_ref, wd_ref, o_ref):
    groups = [slice(r, r + FFN_GROUP_ROWS) for r in range(0, FFN_ROWS, FFN_GROUP_ROWS)]
    hs = [_rmsnorm(x_ref[rows, :], gpre_ref[...]) for rows in groups]
    accs = [jnp.zeros((FFN_GROUP_ROWS, D_MODEL), F32) for _ in groups]
    for c in range(N_FF_CHUNKS):
        cols = slice(c * FF_CHUNK, (c + 1) * FF_CHUNK)
        for i, h in enumerate(hs):
            g = jnp.dot(h, wg_ref[:, cols], preferred_element_type=F32)
            u = jnp.dot(h, wu_ref[:, cols], preferred_element_type=F32)
            a = g * jax.nn.sigmoid(g) * u
            accs[i] = accs[i] + jnp.dot(a, wd_ref[cols, :], preferred_element_type=F32)
    for rows, acc in zip(groups, accs):
        o_ref[rows, :] = x_ref[rows, :] + 0.5 * _rmsnorm(acc, gpost_ref[...])


def _ffn(x2d, norm_g, w_gate, w_up, w_down, layer, half):
    n = x2d.shape[0]
    row_spec = pl.BlockSpec((FFN_ROWS, D_MODEL), lambda i: (i, 0))
    gain = lambda which: pl.BlockSpec((None, None, 1, D_MODEL), lambda i: (layer, which, 0, 0))
    resident = lambda rows, cols: pl.BlockSpec(
        (None, None, rows, cols), lambda i: (layer, half, 0, 0), pipeline_mode=pl.Buffered(1))
    pre, post = (0, 1) if half == 0 else (4, 5)
    return pl.pallas_call(
        _ffn_kernel,
        name="ffn",
        grid=(n // FFN_ROWS,),
        in_specs=[
            row_spec, gain(pre), gain(post),
            resident(D_MODEL, D_FF), resident(D_MODEL, D_FF), resident(D_FF, D_MODEL),
        ],
        out_specs=row_spec,
        out_shape=jax.ShapeDtypeStruct(x2d.shape, F32),
        compiler_params=pltpu.CompilerParams(
            dimension_semantics=("arbitrary",), vmem_limit_bytes=FFN_VMEM_LIMIT),
    )(x2d, norm_g, norm_g, w_gate, w_up, w_down)


def _in_proj_kernel(tiles_per_seq, x_ref, g_ref, w_ref, convw_ref, poolw_ref, pscale_ref,
                    q_ref, k_ref, v_ref, cp_ref, ubuf, pbuf, s2buf, s4buf, s8buf):
    tile_in_seq = pl.program_id(0) % tiles_per_seq

    @pl.when(tile_in_seq == 0)
    def _():
        ubuf[0:HALO, :] = jnp.zeros((HALO, CONV_DIM), F32)
        pbuf[0:HALO, :] = jnp.zeros((HALO, POOL_DIM), F32)

    @pl.when(tile_in_seq != 0)
    def _():
        ubuf[0:HALO, :] = ubuf[PROJ_ROWS:PROJ_ROWS + HALO, :]
        pbuf[0:HALO, :] = pbuf[PROJ_ROWS:PROJ_ROWS + HALO, :]

    starts = list(range(0, PROJ_ROWS, PROJ_GROUP_ROWS))
    mix_cols = slice(3 * ATTN_DIM, IN_PROJ_WIDTH)
    h = _rmsnorm(x_ref[0:PROJ_GROUP_ROWS, :], g_ref[...])
    for gi, r0 in enumerate(starts):
        out_rows = slice(r0, r0 + PROJ_GROUP_ROWS)
        mix_in = jnp.dot(h, w_ref[:, mix_cols], preferred_element_type=F32)
        q = jnp.dot(h, w_ref[:, 0:ATTN_DIM], preferred_element_type=F32)
        q_ref[out_rows, :] = (q * (1.0 / math.sqrt(ATTN_HEAD_DIM))).astype(BF16)
        _mix_rows(r0, tile_in_seq * PROJ_ROWS + r0, mix_in, convw_ref, poolw_ref, pscale_ref,
                  cp_ref, ubuf, pbuf, s2buf, s4buf, s8buf)
        k_ref[out_rows, :] = jnp.dot(h, w_ref[:, ATTN_DIM:2 * ATTN_DIM],
                                     preferred_element_type=F32).astype(BF16)
        h_next = None
        if gi + 1 < len(starts):
            nxt = starts[gi + 1]
            h_next = _rmsnorm(x_ref[nxt:nxt + PROJ_GROUP_ROWS, :], g_ref[...])
        v_ref[out_rows, :] = jnp.dot(h, w_ref[:, 2 * ATTN_DIM:3 * ATTN_DIM],
                                     preferred_element_type=F32).astype(BF16)
        h = h_next


def _mix_rows(r0, seq_pos, mix_in, convw_ref, poolw_ref, pscale_ref,
              cp_ref, ubuf, pbuf, s2buf, s4buf, s8buf):
    rows = PROJ_GROUP_ROWS
    out_rows = slice(r0, r0 + rows)
    gate_b = mix_in[:, 0:CONV_DIM]
    gate_c = mix_in[:, CONV_DIM:2 * CONV_DIM]
    conv_h = mix_in[:, 2 * CONV_DIM:3 * CONV_DIM]
    p = mix_in[:, 3 * CONV_DIM:]

    tok = r0 + HALO
    end = tok + rows
    ubuf[tok:end, :] = gate_c * conv_h
    cw = convw_ref[...]
    y = (cw[0:1, :] * ubuf[tok - 2:end - 2, :]
         + cw[1:2, :] * ubuf[tok - 1:end - 1, :]
         + cw[2:3, :] * ubuf[tok:end, :])
    cp_ref[out_rows, 0:CONV_DIM] = (gate_b * y).astype(BF16)

    pbuf[tok:end, :] = p
    s2buf[r0 + 8:end, :] = pbuf[r0 + 8:end, :] + pbuf[r0 + 7:end - 1, :]
    s4buf[r0 + 16:end, :] = s2buf[r0 + 16:end, :] + s2buf[r0 + 14:end - 2, :]
    s8buf[r0 + 24:end, :] = s4buf[r0 + 24:end, :] + s4buf[r0 + 20:end - 4, :]
    s16 = s8buf[tok:end, :] + s8buf[tok - 8:end - 8, :]
    group = lax.broadcasted_iota(jnp.int32, (rows, POOL_DIM), 1) // POOL_GD
    win_sum = jnp.where(group == 0, s2buf[tok:end, :],
                        jnp.where(group == 1, s4buf[tok:end, :],
                                  jnp.where(group == 2, s8buf[tok:end, :], s16)))
    window = jnp.where(group == 0, POOL_WINDOWS[0],
                       jnp.where(group == 1, POOL_WINDOWS[1],
                                 jnp.where(group == 2, POOL_WINDOWS[2], POOL_WINDOWS[3])))
    pos = seq_pos + lax.broadcasted_iota(jnp.int32, (rows, POOL_DIM), 0)
    count = jnp.minimum(pos + 1, window).astype(F32)
    d = win_sum / count - p
    pooled = jnp.dot(d, poolw_ref[...], preferred_element_type=F32)
    cp_ref[out_rows, CONV_DIM:] = (pooled * pscale_ref[...]).astype(BF16)


def _in_proj(x2d, seq, norm_g, w_in, conv_w, pool_w_bd, pool_scale, layer):
    n = x2d.shape[0]
    tiles_per_seq = seq // PROJ_ROWS
    row = lambda width: pl.BlockSpec((PROJ_ROWS, width), lambda i: (i, 0))
    of_layer = lambda rows, cols, **kw: pl.BlockSpec((None, rows, cols), lambda i: (layer, 0, 0), **kw)
    att = jax.ShapeDtypeStruct((n, ATTN_DIM), BF16)
    buf = lambda width: pltpu.VMEM((HALO + PROJ_ROWS, width), F32)
    return pl.pallas_call(
        functools.partial(_in_proj_kernel, tiles_per_seq),
        name="in_proj",
        grid=(n // PROJ_ROWS,),
        in_specs=[
            row(D_MODEL),
            pl.BlockSpec((None, None, 1, D_MODEL), lambda i: (layer, 2, 0, 0)),
            of_layer(D_MODEL, IN_PROJ_WIDTH, pipeline_mode=pl.Buffered(1)),
            of_layer(CONV_WIDTH, CONV_DIM), of_layer(POOL_DIM, POOL_DIM), of_layer(1, POOL_DIM),
        ],
        out_specs=[row(ATTN_DIM), row(ATTN_DIM), row(ATTN_DIM), row(CONV_DIM + POOL_DIM)],
        out_shape=[att, att, att, jax.ShapeDtypeStruct((n, CONV_DIM + POOL_DIM), BF16)],
        scratch_shapes=[buf(CONV_DIM), buf(POOL_DIM), buf(POOL_DIM), buf(POOL_DIM), buf(POOL_DIM)],
        compiler_params=pltpu.CompilerParams(
            dimension_semantics=("arbitrary",), vmem_limit_bytes=VMEM_LIMIT),
    )(x2d, norm_g, w_in, conv_w, pool_w_bd, pool_scale)


def _attn_scores(qm, k2):
    return lax.dot_general(qm, k2, (((1,), (1,)), ((), ())), preferred_element_type=F32)


def _attn_block(z, v2, carry, neg_tri, mask):
    neg_abs = lax.bitcast_convert_type(
        lax.bitcast_convert_type(z, jnp.uint32) | jnp.uint32(0x80000000), F32)
    soft = jnp.log(1.0 + jnp.exp(neg_abs))
    drop = jnp.maximum(z, 0.0) + soft
    log_beta = z - drop
    if mask is not None:
        drop = jnp.where(mask, drop, 0.0)
    between = jnp.dot(drop.astype(BF16), neg_tri, preferred_element_type=F32)
    a = jnp.exp(log_beta + between + carry)
    if mask is not None:
        a = jnp.where(mask, a, 0.0)
    out = jnp.dot(a.astype(BF16), v2, preferred_element_type=F32)
    new_carry = carry + between[:, 0:1] - drop[:, 0:1]
    return out, new_carry


def _attn_out_kernel(x_ref, q_ref, k_ref, v_ref, cp_ref, wout_ref, g_ref, o_ref,
                     qm_ref, mix_ref, acc_ref, carry_ref):
    def tile(t, _):
        _attn_tile(pl.program_id(1) * ATT_TILES + t, t * ATT_Q,
                   q_ref, k_ref, v_ref, cp_ref, qm_ref, mix_ref, acc_ref, carry_ref)
        return 0

    lax.fori_loop(0, ATT_TILES, tile, 0)
    m = jnp.dot(mix_ref[...], wout_ref[...], preferred_element_type=F32)
    o_ref[0] = x_ref[0] + _rmsnorm(m, g_ref[...])


def _attn_tile(qi, row0, q_ref, k_ref, v_ref, cp_ref, qm_ref, mix_ref, acc_ref, carry_ref):
    n_pairs = ATTN_HEADS // 2
    early, late = slice(0, EARLY_ROWS), slice(EARLY_ROWS, ATT_Q)
    n_early = 2 * EARLY_ROWS
    row = lax.broadcasted_iota(jnp.int32, (ATT_Q, ATT_K), 0)
    col = lax.broadcasted_iota(jnp.int32, (ATT_Q, ATT_K), 1)
    neg_tri = jnp.where(row > col, -1.0, 0.0).astype(BF16)

    rows_early = lax.broadcasted_iota(jnp.int32, (EARLY_ROWS, ATT_K), 0)
    rows_late = EARLY_ROWS + lax.broadcasted_iota(jnp.int32, (ATT_Q - EARLY_ROWS, ATT_K), 0)
    tile_row = jnp.concatenate([rows_early, rows_early, rows_late, rows_late], axis=0)
    causal = lax.broadcasted_iota(jnp.int32, (2 * ATT_Q, ATT_K), 1) < tile_row
    first_head = lax.broadcasted_iota(jnp.int32, (ATT_Q, LANES), 1) < ATTN_HEAD_DIM
    pair_lanes = [slice(hp * LANES, (hp + 1) * LANES) for hp in range(n_pairs)]

    for hp in range(n_pairs):
        q2 = q_ref[0, pl.ds(pl.multiple_of(row0, ATT_Q), ATT_Q), pair_lanes[hp]]
        head0 = jnp.where(first_head, q2, jnp.zeros_like(q2))
        head1 = jnp.where(first_head, jnp.zeros_like(q2), q2)
        qm_ref[hp] = jnp.concatenate([head0[early], head1[early], head0[late], head1[late]], axis=0)

    def scores(hp, blocks_back, rows):
        start = pl.multiple_of((qi - blocks_back) * ATT_K, ATT_K)
        return _attn_scores(qm_ref[hp, rows, :], k_ref[0, pl.ds(start, ATT_K), pair_lanes[hp]])

    def sweep(items):
        zs = [scores(*item[:3]) for item in items[:QK_AHEAD]]
        for n, (hp, blocks_back, rows, is_diagonal) in enumerate(items):
            if n + QK_AHEAD < len(items):
                zs.append(scores(*items[n + QK_AHEAD][:3]))
            z = zs[n]
            start = pl.multiple_of((qi - blocks_back) * ATT_K, ATT_K)
            v2 = v_ref[0, pl.ds(start, ATT_K), pair_lanes[hp]]
            if is_diagonal:
                out, carry = _attn_block(z, v2, jnp.zeros((2 * ATT_Q, 1), F32), neg_tri, causal)
                acc_ref[hp, rows, :] = out
            else:
                out, carry = _attn_block(z, v2, carry_ref[hp, rows, :], neg_tri, None)
                acc_ref[hp, rows, :] += out
            carry_ref[hp, rows, :] = carry

    everything, early_rows, late_rows = slice(0, 2 * ATT_Q), slice(0, n_early), slice(n_early, 2 * ATT_Q)

    def leading_blocks(with_previous):
        items = []
        for hp in range(n_pairs):
            items.append((hp, 0, everything, True))
            if with_previous:
                items.append((hp, 1, early_rows, False))
        sweep(items)

    pl.when(qi == 0)(lambda: leading_blocks(False))
    pl.when(qi > 0)(lambda: leading_blocks(True))

    def weight_left(rows):
        return jnp.max(carry_ref[:, rows, :]) > UNDERFLOW_LOG

    def accumulate(blocks_back, rows):
        sweep([(hp, blocks_back, rows, False) for hp in range(n_pairs)])

    late_left = weight_left(late_rows)
    early_left = weight_left(early_rows)
    pl.when(jnp.logical_and(qi > 0, late_left))(lambda: accumulate(1, late_rows))

    def body(state):
        blocks_back, _ = state
        accumulate(blocks_back, everything)
        return blocks_back + 1, weight_left(everything)

    lax.while_loop(lambda state: jnp.logical_and(state[0] <= qi, state[1]), body,
                   (jnp.int32(2), jnp.logical_or(early_left, late_left)))

    for hp in range(n_pairs):
        for rows, first, second in ((early, slice(0, EARLY_ROWS), slice(EARLY_ROWS, n_early)),
                                    (late, slice(n_early, n_early + ATT_Q - EARLY_ROWS),
                                     slice(n_early + ATT_Q - EARLY_ROWS, 2 * ATT_Q))):
            head0_lanes = lax.broadcasted_iota(jnp.int32, (rows.stop - rows.start, LANES), 1) < ATTN_HEAD_DIM
            out_rows = pl.ds(pl.multiple_of(row0 + rows.start, ATT_Q - EARLY_ROWS), rows.stop - rows.start)
            mix_ref[out_rows, pair_lanes[hp]] = jnp.where(
                head0_lanes, acc_ref[hp, first, :], acc_ref[hp, second, :])
    tile_rows = pl.ds(pl.multiple_of(row0, ATT_Q), ATT_Q)
    mix_ref[tile_rows, ATTN_DIM:] = cp_ref[0, tile_rows, :].astype(F32)


def _attn_out(x3d, q, k, v, cp, w_out, norm_g, layer):
    b, s, _ = x3d.shape
    tile = lambda width: pl.BlockSpec((1, ATT_TILES * ATT_Q, width), lambda bi, qi: (bi, qi, 0))
    seq = lambda width: pl.BlockSpec((1, s, width), lambda bi, qi: (bi, 0, 0))
    return pl.pallas_call(
        _attn_out_kernel,
        name="attn_out",
        grid=(b, s // (ATT_TILES * ATT_Q)),
        in_specs=[
            tile(D_MODEL), tile(ATTN_DIM), seq(ATTN_DIM), seq(ATTN_DIM), tile(CONV_DIM + POOL_DIM),
            pl.BlockSpec((None, D_MODEL, D_MODEL), lambda bi, qi: (layer, 0, 0)),
            pl.BlockSpec((None, None, 1, D_MODEL), lambda bi, qi: (layer, 3, 0, 0)),
        ],
        out_specs=tile(D_MODEL),
        out_shape=jax.ShapeDtypeStruct(x3d.shape, F32),
        scratch_shapes=[
            pltpu.VMEM((ATTN_HEADS // 2, 2 * ATT_Q, LANES), BF16),
            pltpu.VMEM((ATT_TILES * ATT_Q, D_MODEL), F32),
            pltpu.VMEM((ATTN_HEADS // 2, 2 * ATT_Q, LANES), F32),
            pltpu.VMEM((ATTN_HEADS // 2, 2 * ATT_Q, 1), F32),
        ],
        compiler_params=pltpu.CompilerParams(
            dimension_semantics=("arbitrary", "arbitrary"), vmem_limit_bytes=VMEM_LIMIT),
    )(x3d, q, k, v, cp, w_out, norm_g)


def _block_diag(pool_w):
    depth, groups, gd, _ = pool_w.shape
    eye = jnp.eye(groups, dtype=pool_w.dtype)
    return jnp.einsum("lgij,gh->lgihj", pool_w, eye).reshape(depth, groups * gd, groups * gd)


def kernel(x, norm_g, ffn_w_gate, ffn_w_up, ffn_w_down, w_in, conv_w, pool_w, pool_scale, w_out):
    b, s, d = x.shape
    assert d == D_MODEL and s % PROJ_ROWS == 0 and s % (ATT_TILES * ATT_Q) == 0 and (b * s) % FFN_ROWS == 0
    depth = norm_g.shape[0]
    gains = norm_g.reshape(depth, 6, 1, D_MODEL)
    pool_w_bd = _block_diag(pool_w)
    pool_scale = pool_scale.reshape(depth, 1, POOL_DIM)
    to3d = lambda t: t.reshape(b, s, t.shape[-1])
    x2d = x.reshape(b * s, d)
    for l in range(depth):
        x2d = _ffn(x2d, gains, ffn_w_gate, ffn_w_up, ffn_w_down, l, 0)
        q, k, v, cp = _in_proj(x2d, s, gains, w_in, conv_w, pool_w_bd, pool_scale, l)
        x2d = _attn_out(to3d(x2d), to3d(q), to3d(k), to3d(v), to3d(cp), w_out, gains, l).reshape(b * s, d)
        x2d = _ffn(x2d, gains, ffn_w_gate, ffn_w_up, ffn_w_down, l, 1)
    return x2d.reshape(b, s, d)
```

```python
import functools
import math

import jax
import jax.numpy as jnp
from jax import lax
from jax.experimental import pallas as pl
from jax.experimental.pallas import tpu as pltpu

F32 = jnp.float32
BF16 = jnp.bfloat16

D_MODEL = 1024
D_FF = 2816
ATTN_HEADS = 8
ATTN_HEAD_DIM = 64
ATTN_DIM = ATTN_HEADS * ATTN_HEAD_DIM
CONV_DIM = 256
CONV_WIDTH = 3
POOL_WINDOWS = (2, 4, 8, 16)
POOL_DIM = 256
POOL_GD = POOL_DIM // len(POOL_WINDOWS)
IN_PROJ_WIDTH = 3 * ATTN_DIM + 3 * CONV_DIM + POOL_DIM
RMS_EPS = 1e-6

LANES = 128
FF_CHUNK = 256
N_FF_CHUNKS = D_FF // FF_CHUNK
FFN_ROWS = 1024
FFN_GROUP_ROWS = 512
PROJ_ROWS = 1024
PROJ_GROUP_ROWS = 512
HALO = 32
ATT_Q = 256
ATT_K = 256
ATT_TILES = 4
EARLY_ROWS = 192
QK_AHEAD = 2
UNDERFLOW_LOG = -105.0
V7X_VMEM_BYTES = 64 * 1024 * 1024
VMEM_LIMIT = V7X_VMEM_BYTES * 7 // 8
FFN_VMEM_LIMIT = V7X_VMEM_BYTES * 15 // 16


def _rmsnorm(x, g):
    ms = jnp.mean(x * x, axis=-1, keepdims=True)
    return x * lax.rsqrt(ms + RMS_EPS) * g


def _ffn_kernel(x_ref, gpre_ref, gpost_ref, wg_ref, wu_ref, wd_ref, o_ref):
    groups = [slice(r, r + FFN_GROUP_ROWS) for r in range(0, FFN_ROWS, FFN_GROUP_ROWS)]
    hs = [_rmsnorm(x_ref[rows, :], gpre_ref[...]) for rows in groups]
    accs = [jnp.zeros((FFN_GROUP_ROWS, D_MODEL), F32) for _ in groups]
    for c in range(N_FF_CHUNKS):
        cols = slice(c * FF_CHUNK, (c + 1) * FF_CHUNK)
        for i, h in enumerate(hs):
            g = jnp.dot(h, wg_ref[:, cols], preferred_element_type=F32)
            u = jnp.dot(h, wu_ref[:, cols], preferred_element_type=F32)
            a = g * jax.nn.sigmoid(g) * u
            accs[i] = accs[i] + jnp.dot(a, wd_ref[cols, :], preferred_element_type=F32)
    for rows, acc in zip(groups, accs):
        o_ref[rows, :] = x_ref[rows, :] + 0.5 * _rmsnorm(acc, gpost_ref[...])


def _ffn(x2d, norm_g, w_gate, w_up, w_down, layer, half):
    n = x2d.shape[0]
    row_spec = pl.BlockSpec((FFN_ROWS, D_MODEL), lambda i: (i, 0))
    gain = lambda which: pl.BlockSpec((None, None, 1, D_MODEL), lambda i: (layer, which, 0, 0))
    resident = lambda rows, cols: pl.BlockSpec(
        (None, None, rows, cols), lambda i: (layer, half, 0, 0), pipeline_mode=pl.Buffered(1))
    pre, post = (0, 1) if half == 0 else (4, 5)
    return pl.pallas_call(
        _ffn_kernel,
        name="ffn",
        grid=(n // FFN_ROWS,),
        in_specs=[
            row_spec, gain(pre), gain(post),
            resident(D_MODEL, D_FF), resident(D_MODEL, D_FF), resident(D_FF, D_MODEL),
        ],
        out_specs=row_spec,
        out_shape=jax.ShapeDtypeStruct(x2d.shape, F32),
        compiler_params=pltpu.CompilerParams(
            dimension_semantics=("arbitrary",), vmem_limit_bytes=FFN_VMEM_LIMIT),
    )(x2d, norm_g, norm_g, w_gate, w_up, w_down)


def _in_proj_kernel(tiles_per_seq, x_ref, g_ref, w_ref, convw_ref, poolw_ref, pscale_ref,
                    q_ref, k_ref, v_ref, cp_ref, ubuf, pbuf, s2buf, s4buf, s8buf):
    tile_in_seq = pl.program_id(0) % tiles_per_seq

    @pl.when(tile_in_seq == 0)
    def _():
        ubuf[0:HALO, :] = jnp.zeros((HALO, CONV_DIM), F32)
        pbuf[0:HALO, :] = jnp.zeros((HALO, POOL_DIM), F32)

    @pl.when(tile_in_seq != 0)
    def _():
        ubuf[0:HALO, :] = ubuf[PROJ_ROWS:PROJ_ROWS + HALO, :]
        pbuf[0:HALO, :] = pbuf[PROJ_ROWS:PROJ_ROWS + HALO, :]

    starts = list(range(0, PROJ_ROWS, PROJ_GROUP_ROWS))
    mix_cols = slice(3 * ATTN_DIM, IN_PROJ_WIDTH)
    h = _rmsnorm(x_ref[0:PROJ_GROUP_ROWS, :], g_ref[...])
    for gi, r0 in enumerate(starts):
        out_rows = slice(r0, r0 + PROJ_GROUP_ROWS)
        mix_in = jnp.dot(h, w_ref[:, mix_cols], preferred_element_type=F32)
        q = jnp.dot(h, w_ref[:, 0:ATTN_DIM], preferred_element_type=F32)
        q_ref[out_rows, :] = (q * (1.0 / math.sqrt(ATTN_HEAD_DIM))).astype(BF16)
        _mix_rows(r0, tile_in_seq * PROJ_ROWS + r0, mix_in, convw_ref, poolw_ref, pscale_ref,
                  cp_ref, ubuf, pbuf, s2buf, s4buf, s8buf)
        k_ref[out_rows, :] = jnp.dot(h, w_ref[:, ATTN_DIM:2 * ATTN_DIM],
                                     preferred_element_type=F32).astype(BF16)
        h_next = None
        if gi + 1 < len(starts):
            nxt = starts[gi + 1]
            h_next = _rmsnorm(x_ref[nxt:nxt + PROJ_GROUP_ROWS, :], g_ref[...])
        v_ref[out_rows, :] = jnp.dot(h, w_ref[:, 2 * ATTN_DIM:3 * ATTN_DIM],
                                     preferred_element_type=F32).astype(BF16)
        h = h_next


def _mix_rows(r0, seq_pos, mix_in, convw_ref, poolw_ref, pscale_ref,
              cp_ref, ubuf, pbuf, s2buf, s4buf, s8buf):
    rows = PROJ_GROUP_ROWS
    out_rows = slice(r0, r0 + rows)
    gate_b = mix_in[:, 0:CONV_DIM]
    gate_c = mix_in[:, CONV_DIM:2 * CONV_DIM]
    conv_h = mix_in[:, 2 * CONV_DIM:3 * CONV_DIM]
    p = mix_in[:, 3 * CONV_DIM:]

    tok = r0 + HALO
    end = tok + rows
    ubuf[tok:end, :] = gate_c * conv_h
    cw = convw_ref[...]
    y = (cw[0:1, :] * ubuf[tok - 2:end - 2, :]
         + cw[1:2, :] * ubuf[tok - 1:end - 1, :]
         + cw[2:3, :] * ubuf[tok:end, :])
    cp_ref[out_rows, 0:CONV_DIM] = (gate_b * y).astype(BF16)

    pbuf[tok:end, :] = p
    s2buf[r0 + 8:end, :] = pbuf[r0 + 8:end, :] + pbuf[r0 + 7:end - 1, :]
    s4buf[r0 + 16:end, :] = s2buf[r0 + 16:end, :] + s2buf[r0 + 14:end - 2, :]
    s8buf[r0 + 24:end, :] = s4buf[r0 + 24:end, :] + s4buf[r0 + 20:end - 4, :]
    s16 = s8buf[tok:end, :] + s8buf[tok - 8:end - 8, :]
    group = lax.broadcasted_iota(jnp.int32, (rows, POOL_DIM), 1) // POOL_GD
    win_sum = jnp.where(group == 0, s2buf[tok:end, :],
                        jnp.where(group == 1, s4buf[tok:end, :],
                                  jnp.where(group == 2, s8buf[tok:end, :], s16)))
    window = jnp.where(group == 0, POOL_WINDOWS[0],
                       jnp.where(group == 1, POOL_WINDOWS[1],
                                 jnp.where(group == 2, POOL_WINDOWS[2], POOL_WINDOWS[3])))
    pos = seq_pos + lax.broadcasted_iota(jnp.int32, (rows, POOL_DIM), 0)
    count = jnp.minimum(pos + 1, window).astype(F32)
    d = win_sum / count - p
    pooled = jnp.dot(d, poolw_ref[...], preferred_element_type=F32)
    cp_ref[out_rows, CONV_DIM:] = (pooled * pscale_ref[...]).astype(BF16)


def _in_proj(x2d, seq, norm_g, w_in, conv_w, pool_w_bd, pool_scale, layer):
    n = x2d.shape[0]
    tiles_per_seq = seq // PROJ_ROWS
    row = lambda width: pl.BlockSpec((PROJ_ROWS, width), lambda i: (i, 0))
    of_layer = lambda rows, cols, **kw: pl.BlockSpec((None, rows, cols), lambda i: (layer, 0, 0), **kw)
    att = jax.ShapeDtypeStruct((n, ATTN_DIM), BF16)
    buf = lambda width: pltpu.VMEM((HALO + PROJ_ROWS, width), F32)
    return pl.pallas_call(
        functools.partial(_in_proj_kernel, tiles_per_seq),
        name="in_proj",
        grid=(n // PROJ_ROWS,),
        in_specs=[
            row(D_MODEL),
            pl.BlockSpec((None, None, 1, D_MODEL), lambda i: (layer, 2, 0, 0)),
            of_layer(D_MODEL, IN_PROJ_WIDTH, pipeline_mode=pl.Buffered(1)),
            of_layer(CONV_WIDTH, CONV_DIM), of_layer(POOL_DIM, POOL_DIM), of_layer(1, POOL_DIM),
        ],
        out_specs=[row(ATTN_DIM), row(ATTN_DIM), row(ATTN_DIM), row(CONV_DIM + POOL_DIM)],
        out_shape=[att, att, att, jax.ShapeDtypeStruct((n, CONV_DIM + POOL_DIM), BF16)],
        scratch_shapes=[buf(CONV_DIM), buf(POOL_DIM), buf(POOL_DIM), buf(POOL_DIM), buf(POOL_DIM)],
        compiler_params=pltpu.CompilerParams(
            dimension_semantics=("arbitrary",), vmem_limit_bytes=VMEM_LIMIT),
    )(x2d, norm_g, w_in, conv_w, pool_w_bd, pool_scale)


def _attn_scores(qm, k2):
    return lax.dot_general(qm, k2, (((1,), (1,)), ((), ())), preferred_element_type=F32)


def _attn_stage1(z, neg_tri, mask):
    neg_abs = lax.bitcast_convert_type(
        lax.bitcast_convert_type(z, jnp.uint32) | jnp.uint32(0x80000000), F32)
    soft = jnp.log(1.0 + jnp.exp(neg_abs))
    drop = jnp.maximum(z, 0.0) + soft
    log_beta = z - drop
    if mask is not None:
        drop = jnp.where(mask, drop, 0.0)
    between = jnp.dot(drop.astype(BF16), neg_tri, preferred_element_type=F32)
    return log_beta, between, drop[:, 0:1]


def _attn_stage2(stage1, v2, carry, mask):
    log_beta, between, drop0 = stage1
    a = jnp.exp(log_beta + between + carry)
    if mask is not None:
        a = jnp.where(mask, a, 0.0)
    out = jnp.dot(a.astype(BF16), v2, preferred_element_type=F32)
    return out, carry + between[:, 0:1] - drop0


def _attn_out_kernel(x_ref, q_ref, k_ref, v_ref, cp_ref, wout_ref, g_ref, o_ref,
                     qm_ref, mix_ref, acc_ref, carry_ref):
    def tile(t, _):
        _attn_tile(pl.program_id(1) * ATT_TILES + t, t * ATT_Q,
                   q_ref, k_ref, v_ref, cp_ref, qm_ref, mix_ref, acc_ref, carry_ref)
        return 0

    lax.fori_loop(0, ATT_TILES, tile, 0)
    m = jnp.dot(mix_ref[...], wout_ref[...], preferred_element_type=F32)
    o_ref[0] = x_ref[0] + _rmsnorm(m, g_ref[...])


def _attn_tile(qi, row0, q_ref, k_ref, v_ref, cp_ref, qm_ref, mix_ref, acc_ref, carry_ref):
    n_pairs = ATTN_HEADS // 2
    early, late = slice(0, EARLY_ROWS), slice(EARLY_ROWS, ATT_Q)
    n_early = 2 * EARLY_ROWS
    row = lax.broadcasted_iota(jnp.int32, (ATT_Q, ATT_K), 0)
    col = lax.broadcasted_iota(jnp.int32, (ATT_Q, ATT_K), 1)
    neg_tri = jnp.where(row > col, -1.0, 0.0).astype(BF16)

    rows_early = lax.broadcasted_iota(jnp.int32, (EARLY_ROWS, ATT_K), 0)
    rows_late = EARLY_ROWS + lax.broadcasted_iota(jnp.int32, (ATT_Q - EARLY_ROWS, ATT_K), 0)
    tile_row = jnp.concatenate([rows_early, rows_early, rows_late, rows_late], axis=0)
    causal = lax.broadcasted_iota(jnp.int32, (2 * ATT_Q, ATT_K), 1) < tile_row
    first_head = lax.broadcasted_iota(jnp.int32, (ATT_Q, LANES), 1) < ATTN_HEAD_DIM
    pair_lanes = [slice(hp * LANES, (hp + 1) * LANES) for hp in range(n_pairs)]

    for hp in range(n_pairs):
        q2 = q_ref[0, pl.ds(pl.multiple_of(row0, ATT_Q), ATT_Q), pair_lanes[hp]]
        head0 = jnp.where(first_head, q2, jnp.zeros_like(q2))
        head1 = jnp.where(first_head, jnp.zeros_like(q2), q2)
        qm_ref[hp] = jnp.concatenate([head0[early], head1[early], head0[late], head1[late]], axis=0)

    def scores(hp, blocks_back, rows):
        start = pl.multiple_of((qi - blocks_back) * ATT_K, ATT_K)
        return _attn_scores(qm_ref[hp, rows, :], k_ref[0, pl.ds(start, ATT_K), pair_lanes[hp]])

    def sweep(items):
        masks = [causal if item[3] else None for item in items]
        zs = [scores(*item[:3]) for item in items[:QK_AHEAD]]
        stage1 = {0: _attn_stage1(zs[0], neg_tri, masks[0])}
        for n, (hp, blocks_back, rows, is_diagonal) in enumerate(items):
            if n + QK_AHEAD < len(items):
                zs.append(scores(*items[n + QK_AHEAD][:3]))
            if n + 1 < len(items):
                stage1[n + 1] = _attn_stage1(zs[n + 1], neg_tri, masks[n + 1])
            start = pl.multiple_of((qi - blocks_back) * ATT_K, ATT_K)
            v2 = v_ref[0, pl.ds(start, ATT_K), pair_lanes[hp]]
            if is_diagonal:
                out, carry = _attn_stage2(stage1.pop(n), v2, jnp.zeros((2 * ATT_Q, 1), F32), causal)
                acc_ref[hp, rows, :] = out
            else:
                out, carry = _attn_stage2(stage1.pop(n), v2, carry_ref[hp, rows, :], None)
                acc_ref[hp, rows, :] += out
            carry_ref[hp, rows, :] = carry

    everything, early_rows, late_rows = slice(0, 2 * ATT_Q), slice(0, n_early), slice(n_early, 2 * ATT_Q)

    def leading_blocks(with_previous):
        items = []
        for hp in range(n_pairs):
            items.append((hp, 0, everything, True))
            if with_previous:
                items.append((hp, 1, early_rows, False))
        sweep(items)

    pl.when(qi == 0)(lambda: leading_blocks(False))
    pl.when(qi > 0)(lambda: leading_blocks(True))

    def weight_left(rows):
        return jnp.max(carry_ref[:, rows, :]) > UNDERFLOW_LOG

    def accumulate(blocks_back, rows):
        sweep([(hp, blocks_back, rows, False) for hp in range(n_pairs)])

    late_left = weight_left(late_rows)
    early_left = weight_left(early_rows)
    pl.when(jnp.logical_and(qi > 0, late_left))(lambda: accumulate(1, late_rows))

    def body(state):
        blocks_back, _ = state
        accumulate(blocks_back, everything)
        return blocks_back + 1, weight_left(everything)

    lax.while_loop(lambda state: jnp.logical_and(state[0] <= qi, state[1]), body,
                   (jnp.int32(2), jnp.logical_or(early_left, late_left)))

    for hp in range(n_pairs):
        for rows, first, second in ((early, slice(0, EARLY_ROWS), slice(EARLY_ROWS, n_early)),
                                    (late, slice(n_early, n_early + ATT_Q - EARLY_ROWS),
                                     slice(n_early + ATT_Q - EARLY_ROWS, 2 * ATT_Q))):
            head0_lanes = lax.broadcasted_iota(jnp.int32, (rows.stop - rows.start, LANES), 1) < ATTN_HEAD_DIM
            out_rows = pl.ds(pl.multiple_of(row0 + rows.start, ATT_Q - EARLY_ROWS), rows.stop - rows.start)
            mix_ref[out_rows, pair_lanes[hp]] = jnp.where(
                head0_lanes, acc_ref[hp, first, :], acc_ref[hp, second, :])
    tile_rows = pl.ds(pl.multiple_of(row0, ATT_Q), ATT_Q)
    mix_ref[tile_rows, ATTN_DIM:] = cp_ref[0, tile_rows, :].astype(F32)


def _attn_out(x3d, q, k, v, cp, w_out, norm_g, layer):
    b, s, _ = x3d.shape
    tile = lambda width: pl.BlockSpec((1, ATT_TILES * ATT_Q, width), lambda bi, qi: (bi, qi, 0))
    seq = lambda width: pl.BlockSpec((1, s, width), lambda bi, qi: (bi, 0, 0))
    return pl.pallas_call(
        _attn_out_kernel,
        name="attn_out",
        grid=(b, s // (ATT_TILES * ATT_Q)),
        in_specs=[
            tile(D_MODEL), tile(ATTN_DIM), seq(ATTN_DIM), seq(ATTN_DIM), tile(CONV_DIM + POOL_DIM),
            pl.BlockSpec((None, D_MODEL, D_MODEL), lambda bi, qi: (layer, 0, 0)),
            pl.BlockSpec((None, None, 1, D_MODEL), lambda bi, qi: (layer, 3, 0, 0)),
        ],
        out_specs=tile(D_MODEL),
        out_shape=jax.ShapeDtypeStruct(x3d.shape, F32),
        scratch_shapes=[
            pltpu.VMEM((ATTN_HEADS // 2, 2 * ATT_Q, LANES), BF16),
            pltpu.VMEM((ATT_TILES * ATT_Q, D_MODEL), F32),
            pltpu.VMEM((ATTN_HEADS // 2, 2 * ATT_Q, LANES), F32),
            pltpu.VMEM((ATTN_HEADS // 2, 2 * ATT_Q, 1), F32),
        ],
        compiler_params=pltpu.CompilerParams(
            dimension_semantics=("arbitrary", "arbitrary"), vmem_limit_bytes=VMEM_LIMIT),
    )(x3d, q, k, v, cp, w_out, norm_g)


def _block_diag(pool_w):
    depth, groups, gd, _ = pool_w.shape
    eye = jnp.eye(groups, dtype=pool_w.dtype)
    return jnp.einsum("lgij,gh->lgihj", pool_w, eye).reshape(depth, groups * gd, groups * gd)


def kernel(x, norm_g, ffn_w_gate, ffn_w_up, ffn_w_down, w_in, conv_w, pool_w, pool_scale, w_out):
    b, s, d = x.shape
    assert d == D_MODEL and s % PROJ_ROWS == 0 and s % (ATT_TILES * ATT_Q) == 0 and (b * s) % FFN_ROWS == 0
    depth = norm_g.shape[0]
    gains = norm_g.reshape(depth, 6, 1, D_MODEL)
    pool_w_bd = _block_diag(pool_w)
    pool_scale = pool_scale.reshape(depth, 1, POOL_DIM)
    to3d = lambda t: t.reshape(b, s, t.shape[-1])
    x2d = x.reshape(b * s, d)
    for l in range(depth):
        x2d = _ffn(x2d, gains, ffn_w_gate, ffn_w_up, ffn_w_down, l, 0)
        q, k, v, cp = _in_proj(x2d, s, gains, w_in, conv_w, pool_w_bd, pool_scale, l)
        x2d = _attn_out(to3d(x2d), to3d(q), to3d(k), to3d(v), to3d(cp), w_out, gains, l).reshape(b * s, d)
        x2d = _ffn(x2d, gains, ffn_w_gate, ffn_w_up, ffn_w_down, l, 1)
    return x2d.reshape(b, s, d)
```

```python
import functools
import math

import jax
import jax.numpy as jnp
from jax import lax
from jax.experimental import pallas as pl
from jax.experimental.pallas import tpu as pltpu

F32 = jnp.float32
BF16 = jnp.bfloat16

D_MODEL = 1024
D_FF = 2816
ATTN_HEADS = 8
ATTN_HEAD_DIM = 64
ATTN_DIM = ATTN_HEADS * ATTN_HEAD_DIM
CONV_DIM = 256
CONV_WIDTH = 3
POOL_WINDOWS = (2, 4, 8, 16)
POOL_DIM = 256
POOL_GD = POOL_DIM // len(POOL_WINDOWS)
IN_PROJ_WIDTH = 3 * ATTN_DIM + 3 * CONV_DIM + POOL_DIM
RMS_EPS = 1e-6

LANES = 128
FF_CHUNK = 256
N_FF_CHUNKS = D_FF // FF_CHUNK
FFN_ROWS = 1024
FFN_GROUP_ROWS = 512
PROJ_ROWS = 1024
PROJ_GROUP_ROWS = 512
HALO = 32
ATT_Q = 256
ATT_K = 256
ATT_TILES = 4
EARLY_ROWS = 192
QK_AHEAD = 2
UNDERFLOW_LOG = -105.0
V7X_VMEM_BYTES = 64 * 1024 * 1024
VMEM_LIMIT = V7X_VMEM_BYTES * 7 // 8
FFN_VMEM_LIMIT = V7X_VMEM_BYTES * 15 // 16


def _rmsnorm(x, g):
    ms = jnp.mean(x * x, axis=-1, keepdims=True)
    return x * lax.rsqrt(ms + RMS_EPS) * g


def _ffn_kernel(x_ref, gpre_ref, gpost_ref, wg_ref, wu_ref, wd_ref, o_ref):
    groups = [slice(r, r + FFN_GROUP_ROWS) for r in range(0, FFN_ROWS, FFN_GROUP_ROWS)]
    hs = [_rmsnorm(x_ref[rows, :], gpre_ref[...]) for rows in groups]
    accs = [jnp.zeros((FFN_GROUP_ROWS, D_MODEL), F32) for _ in groups]
    for c in range(N_FF_CHUNKS):
        cols = slice(c * FF_CHUNK, (c + 1) * FF_CHUNK)
        for i, h in enumerate(hs):
            g = jnp.dot(h, wg_ref[:, cols], preferred_element_type=F32)
            u = jnp.dot(h, wu_ref[:, cols], preferred_element_type=F32)
            a = g * jax.nn.sigmoid(g) * u
            accs[i] = accs[i] + jnp.dot(a, wd_ref[cols, :], preferred_element_type=F32)
    for rows, acc in zip(groups, accs):
        o_ref[rows, :] = x_ref[rows, :] + 0.5 * _rmsnorm(acc, gpost_ref[...])


def _ffn(x2d, norm_g, w_gate, w_up, w_down, layer, half):
    n = x2d.shape[0]
    row_spec = pl.BlockSpec((FFN_ROWS, D_MODEL), lambda i: (i, 0))
    gain = lambda which: pl.BlockSpec((None, None, 1, D_MODEL), lambda i: (layer, which, 0, 0))
    resident = lambda rows, cols: pl.BlockSpec(
        (None, None, rows, cols), lambda i: (layer, half, 0, 0), pipeline_mode=pl.Buffered(1))
    pre, post = (0, 1) if half == 0 else (4, 5)
    return pl.pallas_call(
        _ffn_kernel,
        name="ffn",
        grid=(n // FFN_ROWS,),
        in_specs=[
            row_spec, gain(pre), gain(post),
            resident(D_MODEL, D_FF), resident(D_MODEL, D_FF), resident(D_FF, D_MODEL),
        ],
        out_specs=row_spec,
        out_shape=jax.ShapeDtypeStruct(x2d.shape, F32),
        compiler_params=pltpu.CompilerParams(
            dimension_semantics=("arbitrary",), vmem_limit_bytes=FFN_VMEM_LIMIT),
    )(x2d, norm_g, norm_g, w_gate, w_up, w_down)


def _in_proj_kernel(tiles_per_seq, x_ref, g_ref, w_ref, convw_ref, poolw_ref, pscale_ref,
                    q_ref, k_ref, v_ref, cp_ref, ubuf, pbuf, s2buf, s4buf, s8buf):
    tile_in_seq = pl.program_id(0) % tiles_per_seq

    @pl.when(tile_in_seq == 0)
    def _():
        ubuf[0:HALO, :] = jnp.zeros((HALO, CONV_DIM), F32)
        pbuf[0:HALO, :] = jnp.zeros((HALO, POOL_DIM), F32)

    @pl.when(tile_in_seq != 0)
    def _():
        ubuf[0:HALO, :] = ubuf[PROJ_ROWS:PROJ_ROWS + HALO, :]
        pbuf[0:HALO, :] = pbuf[PROJ_ROWS:PROJ_ROWS + HALO, :]

    starts = list(range(0, PROJ_ROWS, PROJ_GROUP_ROWS))
    mix_cols = slice(3 * ATTN_DIM, IN_PROJ_WIDTH)
    h = _rmsnorm(x_ref[0:PROJ_GROUP_ROWS, :], g_ref[...])
    for gi, r0 in enumerate(starts):
        out_rows = slice(r0, r0 + PROJ_GROUP_ROWS)
        mix_in = jnp.dot(h, w_ref[:, mix_cols], preferred_element_type=F32)
        q = jnp.dot(h, w_ref[:, 0:ATTN_DIM], preferred_element_type=F32)
        q_ref[out_rows, :] = (q * (1.0 / math.sqrt(ATTN_HEAD_DIM))).astype(BF16)
        _mix_rows(r0, tile_in_seq * PROJ_ROWS + r0, mix_in, convw_ref, poolw_ref, pscale_ref,
                  cp_ref, ubuf, pbuf, s2buf, s4buf, s8buf)
        k_ref[out_rows, :] = jnp.dot(h, w_ref[:, ATTN_DIM:2 * ATTN_DIM],
                                     preferred_element_type=F32).astype(BF16)
        h_next = None
        if gi + 1 < len(starts):
            nxt = starts[gi + 1]
            h_next = _rmsnorm(x_ref[nxt:nxt + PROJ_GROUP_ROWS, :], g_ref[...])
        v_ref[out_rows, :] = jnp.dot(h, w_ref[:, 2 * ATTN_DIM:3 * ATTN_DIM],
                                     preferred_element_type=F32).astype(BF16)
        h = h_next


def _mix_rows(r0, seq_pos, mix_in, convw_ref, poolw_ref, pscale_ref,
              cp_ref, ubuf, pbuf, s2buf, s4buf, s8buf):
    rows = PROJ_GROUP_ROWS
    out_rows = slice(r0, r0 + rows)
    gate_b = mix_in[:, 0:CONV_DIM]
    gate_c = mix_in[:, CONV_DIM:2 * CONV_DIM]
    conv_h = mix_in[:, 2 * CONV_DIM:3 * CONV_DIM]
    p = mix_in[:, 3 * CONV_DIM:]

    tok = r0 + HALO
    end = tok + rows
    ubuf[tok:end, :] = gate_c * conv_h
    cw = convw_ref[...]
    y = (cw[0:1, :] * ubuf[tok - 2:end - 2, :]
         + cw[1:2, :] * ubuf[tok - 1:end - 1, :]
         + cw[2:3, :] * ubuf[tok:end, :])
    cp_ref[out_rows, 0:CONV_DIM] = (gate_b * y).astype(BF16)

    pbuf[tok:end, :] = p
    s2buf[r0 + 8:end, :] = pbuf[r0 + 8:end, :] + pbuf[r0 + 7:end - 1, :]
    s4buf[r0 + 16:end, :] = s2buf[r0 + 16:end, :] + s2buf[r0 + 14:end - 2, :]
    s8buf[r0 + 24:end, :] = s4buf[r0 + 24:end, :] + s4buf[r0 + 20:end - 4, :]
    s16 = s8buf[tok:end, :] + s8buf[tok - 8:end - 8, :]
    group = lax.broadcasted_iota(jnp.int32, (rows, POOL_DIM), 1) // POOL_GD
    win_sum = jnp.where(group == 0, s2buf[tok:end, :],
                        jnp.where(group == 1, s4buf[tok:end, :],
                                  jnp.where(group == 2, s8buf[tok:end, :], s16)))
    window = jnp.where(group == 0, POOL_WINDOWS[0],
                       jnp.where(group == 1, POOL_WINDOWS[1],
                                 jnp.where(group == 2, POOL_WINDOWS[2], POOL_WINDOWS[3])))
    pos = seq_pos + lax.broadcasted_iota(jnp.int32, (rows, POOL_DIM), 0)
    count = jnp.minimum(pos + 1, window).astype(F32)
    d = win_sum / count - p
    pooled = jnp.dot(d, poolw_ref[...], preferred_element_type=F32)
    cp_ref[out_rows, CONV_DIM:] = (pooled * pscale_ref[...]).astype(BF16)


def _in_proj(x2d, seq, norm_g, w_in, conv_w, pool_w_bd, pool_scale, layer):
    n = x2d.shape[0]
    tiles_per_seq = seq // PROJ_ROWS
    row = lambda width: pl.BlockSpec((PROJ_ROWS, width), lambda i: (i, 0))
    of_layer = lambda rows, cols, **kw: pl.BlockSpec((None, rows, cols), lambda i: (layer, 0, 0), **kw)
    att = jax.ShapeDtypeStruct((n, ATTN_DIM), BF16)
    buf = lambda width: pltpu.VMEM((HALO + PROJ_ROWS, width), F32)
    return pl.pallas_call(
        functools.partial(_in_proj_kernel, tiles_per_seq),
        name="in_proj",
        grid=(n // PROJ_ROWS,),
        in_specs=[
            row(D_MODEL),
            pl.BlockSpec((None, None, 1, D_MODEL), lambda i: (layer, 2, 0, 0)),
            of_layer(D_MODEL, IN_PROJ_WIDTH, pipeline_mode=pl.Buffered(1)),
            of_layer(CONV_WIDTH, CONV_DIM), of_layer(POOL_DIM, POOL_DIM), of_layer(1, POOL_DIM),
        ],
        out_specs=[row(ATTN_DIM), row(ATTN_DIM), row(ATTN_DIM), row(CONV_DIM + POOL_DIM)],
        out_shape=[att, att, att, jax.ShapeDtypeStruct((n, CONV_DIM + POOL_DIM), BF16)],
        scratch_shapes=[buf(CONV_DIM), buf(POOL_DIM), buf(POOL_DIM), buf(POOL_DIM), buf(POOL_DIM)],
        compiler_params=pltpu.CompilerParams(
            dimension_semantics=("arbitrary",), vmem_limit_bytes=VMEM_LIMIT),
    )(x2d, norm_g, w_in, conv_w, pool_w_bd, pool_scale)


def _attn_scores(qm, k2):
    return lax.dot_general(qm, k2, (((1,), (1,)), ((), ())), preferred_element_type=F32)


def _attn_stage1(z, neg_tri, mask):
    neg_abs = lax.bitcast_convert_type(
        lax.bitcast_convert_type(z, jnp.uint32) | jnp.uint32(0x80000000), F32)
    soft = jnp.log(1.0 + jnp.exp(neg_abs))
    drop = jnp.maximum(z, 0.0) + soft
    log_beta = z - drop
    if mask is not None:
        drop = jnp.where(mask, drop, 0.0)
    between = jnp.dot(drop.astype(BF16), neg_tri, preferred_element_type=F32)
    return log_beta, between, drop[:, 0:1]


def _attn_stage2(stage1, v2, carry, mask):
    log_beta, between, drop0 = stage1
    a = jnp.exp(log_beta + between + carry)
    if mask is not None:
        a = jnp.where(mask, a, 0.0)
    out = jnp.dot(a.astype(BF16), v2, preferred_element_type=F32)
    return out, carry + between[:, 0:1] - drop0


def _attn_out_kernel(x_ref, q_ref, k_ref, v_ref, cp_ref, wout_ref, g_ref, o_ref,
                     qm_ref, mix_ref, acc_ref, carry_ref):
    def tile(t, _):
        _attn_tile(pl.program_id(1) * ATT_TILES + t, t * ATT_Q,
                   q_ref, k_ref, v_ref, cp_ref, qm_ref, mix_ref, acc_ref, carry_ref)
        return 0

    lax.fori_loop(0, ATT_TILES, tile, 0)
    m = jnp.dot(mix_ref[...], wout_ref[...], preferred_element_type=F32)
    o_ref[0] = x_ref[0] + _rmsnorm(m, g_ref[...])


def _attn_tile(qi, row0, q_ref, k_ref, v_ref, cp_ref, qm_ref, mix_ref, acc_ref, carry_ref):
    n_pairs = ATTN_HEADS // 2
    early, late = slice(0, EARLY_ROWS), slice(EARLY_ROWS, ATT_Q)
    n_early = 2 * EARLY_ROWS
    row = lax.broadcasted_iota(jnp.int32, (ATT_Q, ATT_K), 0)
    col = lax.broadcasted_iota(jnp.int32, (ATT_Q, ATT_K), 1)
    neg_tri = jnp.where(row > col, -1.0, 0.0).astype(BF16)

    rows_early = lax.broadcasted_iota(jnp.int32, (EARLY_ROWS, ATT_K), 0)
    rows_late = EARLY_ROWS + lax.broadcasted_iota(jnp.int32, (ATT_Q - EARLY_ROWS, ATT_K), 0)
    tile_row = jnp.concatenate([rows_early, rows_early, rows_late, rows_late], axis=0)
    causal = lax.broadcasted_iota(jnp.int32, (2 * ATT_Q, ATT_K), 1) < tile_row
    first_head = lax.broadcasted_iota(jnp.int32, (ATT_Q, LANES), 1) < ATTN_HEAD_DIM
    pair_lanes = [slice(hp * LANES, (hp + 1) * LANES) for hp in range(n_pairs)]

    for hp in range(n_pairs):
        q2 = q_ref[0, pl.ds(pl.multiple_of(row0, ATT_Q), ATT_Q), pair_lanes[hp]]
        head0 = jnp.where(first_head, q2, jnp.zeros_like(q2))
        head1 = jnp.where(first_head, jnp.zeros_like(q2), q2)
        qm_ref[hp] = jnp.concatenate([head0[early], head1[early], head0[late], head1[late]], axis=0)

    def scores(hp, blocks_back, rows):
        start = pl.multiple_of((qi - blocks_back) * ATT_K, ATT_K)
        return _attn_scores(qm_ref[hp, rows, :], k_ref[0, pl.ds(start, ATT_K), pair_lanes[hp]])

    def sweep(items):
        masks = [causal if item[3] else None for item in items]
        zs = [scores(*item[:3]) for item in items[:QK_AHEAD]]
        stage1 = {0: _attn_stage1(zs[0], neg_tri, masks[0])}
        for n, (hp, blocks_back, rows, is_diagonal) in enumerate(items):
            if n + QK_AHEAD < len(items):
                zs.append(scores(*items[n + QK_AHEAD][:3]))
            if n + 1 < len(items):
                stage1[n + 1] = _attn_stage1(zs[n + 1], neg_tri, masks[n + 1])
            start = pl.multiple_of((qi - blocks_back) * ATT_K, ATT_K)
            v2 = v_ref[0, pl.ds(start, ATT_K), pair_lanes[hp]]
            if is_diagonal:
                out, carry = _attn_stage2(stage1.pop(n), v2, jnp.zeros((2 * ATT_Q, 1), F32), causal)
                acc_ref[hp, rows, :] = out
            else:
                out, carry = _attn_stage2(stage1.pop(n), v2, carry_ref[hp, rows, :], None)
                acc_ref[hp, rows, :] += out
            carry_ref[hp, rows, :] = carry

    everything, early_rows, late_rows = slice(0, 2 * ATT_Q), slice(0, n_early), slice(n_early, 2 * ATT_Q)

    def leading_blocks(with_previous):
        items = [(hp, 0, everything, True) for hp in range(n_pairs)]
        if with_previous:
            items += [(hp, 1, early_rows, False) for hp in range(n_pairs)]
        sweep(items)

    pl.when(qi == 0)(lambda: leading_blocks(False))
    pl.when(qi > 0)(lambda: leading_blocks(True))

    def weight_left(rows):
        return jnp.max(carry_ref[:, rows, :]) > UNDERFLOW_LOG

    def accumulate(blocks_back, rows):
        sweep([(hp, blocks_back, rows, False) for hp in range(n_pairs)])

    late_left = weight_left(late_rows)
    early_left = weight_left(early_rows)
    pl.when(jnp.logical_and(qi > 0, late_left))(lambda: accumulate(1, late_rows))

    def body(state):
        blocks_back, _ = state
        accumulate(blocks_back, everything)
        return blocks_back + 1, weight_left(everything)

    lax.while_loop(lambda state: jnp.logical_and(state[0] <= qi, state[1]), body,
                   (jnp.int32(2), jnp.logical_or(early_left, late_left)))

    for hp in range(n_pairs):
        for rows, first, second in ((early, slice(0, EARLY_ROWS), slice(EARLY_ROWS, n_early)),
                                    (late, slice(n_early, n_early + ATT_Q - EARLY_ROWS),
                                     slice(n_early + ATT_Q - EARLY_ROWS, 2 * ATT_Q))):
            head0_lanes = lax.broadcasted_iota(jnp.int32, (rows.stop - rows.start, LANES), 1) < ATTN_HEAD_DIM
            out_rows = pl.ds(pl.multiple_of(row0 + rows.start, ATT_Q - EARLY_ROWS), rows.stop - rows.start)
            mix_ref[out_rows, pair_lanes[hp]] = jnp.where(
                head0_lanes, acc_ref[hp, first, :], acc_ref[hp, second, :])
    tile_rows = pl.ds(pl.multiple_of(row0, ATT_Q), ATT_Q)
    mix_ref[tile_rows, ATTN_DIM:] = cp_ref[0, tile_rows, :].astype(F32)


def _attn_out(x3d, q, k, v, cp, w_out, norm_g, layer):
    b, s, _ = x3d.shape
    tile = lambda width: pl.BlockSpec((1, ATT_TILES * ATT_Q, width), lambda bi, qi: (bi, qi, 0))
    seq = lambda width: pl.BlockSpec((1, s, width), lambda bi, qi: (bi, 0, 0))
    return pl.pallas_call(
        _attn_out_kernel,
        name="attn_out",
        grid=(b, s // (ATT_TILES * ATT_Q)),
        in_specs=[
            tile(D_MODEL), tile(ATTN_DIM), seq(ATTN_DIM), seq(ATTN_DIM), tile(CONV_DIM + POOL_DIM),
            pl.BlockSpec((None, D_MODEL, D_MODEL), lambda bi, qi: (layer, 0, 0)),
            pl.BlockSpec((None, None, 1, D_MODEL), lambda bi, qi: (layer, 3, 0, 0)),
        ],
        out_specs=tile(D_MODEL),
        out_shape=jax.ShapeDtypeStruct(x3d.shape, F32),
        scratch_shapes=[
            pltpu.VMEM((ATTN_HEADS // 2, 2 * ATT_Q, LANES), BF16),
            pltpu.VMEM((ATT_TILES * ATT_Q, D_MODEL), F32),
            pltpu.VMEM((ATTN_HEADS // 2, 2 * ATT_Q, LANES), F32),
            pltpu.VMEM((ATTN_HEADS // 2, 2 * ATT_Q, 1), F32),
        ],
        compiler_params=pltpu.CompilerParams(
            dimension_semantics=("arbitrary", "arbitrary"), vmem_limit_bytes=VMEM_LIMIT),
    )(x3d, q, k, v, cp, w_out, norm_g)


def _block_diag(pool_w):
    depth, groups, gd, _ = pool_w.shape
    eye = jnp.eye(groups, dtype=pool_w.dtype)
    return jnp.einsum("lgij,gh->lgihj", pool_w, eye).reshape(depth, groups * gd, groups * gd)


def kernel(x, norm_g, ffn_w_gate, ffn_w_up, ffn_w_down, w_in, conv_w, pool_w, pool_scale, w_out):
    b, s, d = x.shape
    assert d == D_MODEL and s % PROJ_ROWS == 0 and s % (ATT_TILES * ATT_Q) == 0 and (b * s) % FFN_ROWS == 0
    depth = norm_g.shape[0]
    gains = norm_g.reshape(depth, 6, 1, D_MODEL)
    pool_w_bd = _block_diag(pool_w)
    pool_scale = pool_scale.reshape(depth, 1, POOL_DIM)
    to3d = lambda t: t.reshape(b, s, t.shape[-1])
    x2d = x.reshape(b * s, d)
    for l in range(depth):
        x2d = _ffn(x2d, gains, ffn_w_gate, ffn_w_up, ffn_w_down, l, 0)
        q, k, v, cp = _in_proj(x2d, s, gains, w_in, conv_w, pool_w_bd, pool_scale, l)
        x2d = _attn_out(to3d(x2d), to3d(q), to3d(k), to3d(v), to3d(cp), w_out, gains, l).reshape(b * s, d)
        x2d = _ffn(x2d, gains, ffn_w_gate, ffn_w_up, ffn_w_down, l, 1)
    return x2d.reshape(b, s, d)
```

```python
import functools
import math

import jax
import jax.numpy as jnp
from jax import lax
from jax.experimental import pallas as pl
from jax.experimental.pallas import tpu as pltpu

F32 = jnp.float32
BF16 = jnp.bfloat16

D_MODEL = 1024
D_FF = 2816
ATTN_HEADS = 8
ATTN_HEAD_DIM = 64
ATTN_DIM = ATTN_HEADS * ATTN_HEAD_DIM
CONV_DIM = 256
CONV_WIDTH = 3
POOL_WINDOWS = (2, 4, 8, 16)
POOL_DIM = 256
POOL_GD = POOL_DIM // len(POOL_WINDOWS)
IN_PROJ_WIDTH = 3 * ATTN_DIM + 3 * CONV_DIM + POOL_DIM
RMS_EPS = 1e-6

LANES = 128
FF_CHUNK = 256
N_FF_CHUNKS = D_FF // FF_CHUNK
FFN_ROWS = 1024
FFN_GROUP_ROWS = 512
PROJ_ROWS = 1024
PROJ_GROUP_ROWS = 512
HALO = 32
ATT_Q = 256
ATT_K = 256
ATT_TILES = 4
EARLY_ROWS = 176
QK_AHEAD = 2
UNDERFLOW_LOG = -105.0
V7X_VMEM_BYTES = 64 * 1024 * 1024
VMEM_LIMIT = V7X_VMEM_BYTES * 7 // 8
FFN_VMEM_LIMIT = V7X_VMEM_BYTES * 15 // 16


def _rmsnorm(x, g):
    ms = jnp.mean(x * x, axis=-1, keepdims=True)
    return x * lax.rsqrt(ms + RMS_EPS) * g


def _ffn_kernel(x_ref, gpre_ref, gpost_ref, wg_ref, wu_ref, wd_ref, o_ref):
    groups = [slice(r, r + FFN_GROUP_ROWS) for r in range(0, FFN_ROWS, FFN_GROUP_ROWS)]
    hs = [_rmsnorm(x_ref[rows, :], gpre_ref[...]) for rows in groups]
    accs = [jnp.zeros((FFN_GROUP_ROWS, D_MODEL), F32) for _ in groups]
    for c in range(N_FF_CHUNKS):
        cols = slice(c * FF_CHUNK, (c + 1) * FF_CHUNK)
        for i, h in enumerate(hs):
            g = jnp.dot(h, wg_ref[:, cols], preferred_element_type=F32)
            u = jnp.dot(h, wu_ref[:, cols], preferred_element_type=F32)
            a = g * jax.nn.sigmoid(g) * u
            accs[i] = accs[i] + jnp.dot(a, wd_ref[cols, :], preferred_element_type=F32)
    for rows, acc in zip(groups, accs):
        o_ref[rows, :] = x_ref[rows, :] + 0.5 * _rmsnorm(acc, gpost_ref[...])


def _ffn(x2d, norm_g, w_gate, w_up, w_down, layer, half):
    n = x2d.shape[0]
    row_spec = pl.BlockSpec((FFN_ROWS, D_MODEL), lambda i: (i, 0))
    gain = lambda which: pl.BlockSpec((None, None, 1, D_MODEL), lambda i: (layer, which, 0, 0))
    resident = lambda rows, cols: pl.BlockSpec(
        (None, None, rows, cols), lambda i: (layer, half, 0, 0), pipeline_mode=pl.Buffered(1))
    pre, post = (0, 1) if half == 0 else (4, 5)
    return pl.pallas_call(
        _ffn_kernel,
        name="ffn",
        grid=(n // FFN_ROWS,),
        in_specs=[
            row_spec, gain(pre), gain(post),
            resident(D_MODEL, D_FF), resident(D_MODEL, D_FF), resident(D_FF, D_MODEL),
        ],
        out_specs=row_spec,
        out_shape=jax.ShapeDtypeStruct(x2d.shape, F32),
        compiler_params=pltpu.CompilerParams(
            dimension_semantics=("arbitrary",), vmem_limit_bytes=FFN_VMEM_LIMIT),
    )(x2d, norm_g, norm_g, w_gate, w_up, w_down)


def _in_proj_kernel(tiles_per_seq, x_ref, g_ref, w_ref, convw_ref, poolw_ref, pscale_ref,
                    q_ref, k_ref, v_ref, cp_ref, ubuf, pbuf, s2buf, s4buf, s8buf):
    tile_in_seq = pl.program_id(0) % tiles_per_seq

    @pl.when(tile_in_seq == 0)
    def _():
        ubuf[0:HALO, :] = jnp.zeros((HALO, CONV_DIM), F32)
        pbuf[0:HALO, :] = jnp.zeros((HALO, POOL_DIM), F32)

    @pl.when(tile_in_seq != 0)
    def _():
        ubuf[0:HALO, :] = ubuf[PROJ_ROWS:PROJ_ROWS + HALO, :]
        pbuf[0:HALO, :] = pbuf[PROJ_ROWS:PROJ_ROWS + HALO, :]

    starts = list(range(0, PROJ_ROWS, PROJ_GROUP_ROWS))
    mix_cols = slice(3 * ATTN_DIM, IN_PROJ_WIDTH)
    h = _rmsnorm(x_ref[0:PROJ_GROUP_ROWS, :], g_ref[...])
    for gi, r0 in enumerate(starts):
        out_rows = slice(r0, r0 + PROJ_GROUP_ROWS)
        mix_in = jnp.dot(h, w_ref[:, mix_cols], preferred_element_type=F32)
        q = jnp.dot(h, w_ref[:, 0:ATTN_DIM], preferred_element_type=F32)
        q_ref[out_rows, :] = (q * (1.0 / math.sqrt(ATTN_HEAD_DIM))).astype(BF16)
        _mix_rows(r0, tile_in_seq * PROJ_ROWS + r0, mix_in, convw_ref, poolw_ref, pscale_ref,
                  cp_ref, ubuf, pbuf, s2buf, s4buf, s8buf)
        k_ref[out_rows, :] = jnp.dot(h, w_ref[:, ATTN_DIM:2 * ATTN_DIM],
                                     preferred_element_type=F32).astype(BF16)
        h_next = None
        if gi + 1 < len(starts):
            nxt = starts[gi + 1]
            h_next = _rmsnorm(x_ref[nxt:nxt + PROJ_GROUP_ROWS, :], g_ref[...])
        v_ref[out_rows, :] = jnp.dot(h, w_ref[:, 2 * ATTN_DIM:3 * ATTN_DIM],
                                     preferred_element_type=F32).astype(BF16)
        h = h_next


def _mix_rows(r0, seq_pos, mix_in, convw_ref, poolw_ref, pscale_ref,
              cp_ref, ubuf, pbuf, s2buf, s4buf, s8buf):
    rows = PROJ_GROUP_ROWS
    out_rows = slice(r0, r0 + rows)
    gate_b = mix_in[:, 0:CONV_DIM]
    gate_c = mix_in[:, CONV_DIM:2 * CONV_DIM]
    conv_h = mix_in[:, 2 * CONV_DIM:3 * CONV_DIM]
    p = mix_in[:, 3 * CONV_DIM:]

    tok = r0 + HALO
    end = tok + rows
    ubuf[tok:end, :] = gate_c * conv_h
    cw = convw_ref[...]
    y = (cw[0:1, :] * ubuf[tok - 2:end - 2, :]
         + cw[1:2, :] * ubuf[tok - 1:end - 1, :]
         + cw[2:3, :] * ubuf[tok:end, :])
    cp_ref[out_rows, 0:CONV_DIM] = (gate_b * y).astype(BF16)

    pbuf[tok:end, :] = p
    s2buf[r0 + 8:end, :] = pbuf[r0 + 8:end, :] + pbuf[r0 + 7:end - 1, :]
    s4buf[r0 + 16:end, :] = s2buf[r0 + 16:end, :] + s2buf[r0 + 14:end - 2, :]
    s8buf[r0 + 24:end, :] = s4buf[r0 + 24:end, :] + s4buf[r0 + 20:end - 4, :]
    s16 = s8buf[tok:end, :] + s8buf[tok - 8:end - 8, :]
    group = lax.broadcasted_iota(jnp.int32, (rows, POOL_DIM), 1) // POOL_GD
    win_sum = jnp.where(group == 0, s2buf[tok:end, :],
                        jnp.where(group == 1, s4buf[tok:end, :],
                                  jnp.where(group == 2, s8buf[tok:end, :], s16)))
    window = jnp.where(group == 0, POOL_WINDOWS[0],
                       jnp.where(group == 1, POOL_WINDOWS[1],
                                 jnp.where(group == 2, POOL_WINDOWS[2], POOL_WINDOWS[3])))
    pos = seq_pos + lax.broadcasted_iota(jnp.int32, (rows, POOL_DIM), 0)
    count = jnp.minimum(pos + 1, window).astype(F32)
    d = win_sum / count - p
    pooled = jnp.dot(d, poolw_ref[...], preferred_element_type=F32)
    cp_ref[out_rows, CONV_DIM:] = (pooled * pscale_ref[...]).astype(BF16)


def _in_proj(x2d, seq, norm_g, w_in, conv_w, pool_w_bd, pool_scale, layer):
    n = x2d.shape[0]
    tiles_per_seq = seq // PROJ_ROWS
    row = lambda width: pl.BlockSpec((PROJ_ROWS, width), lambda i: (i, 0))
    of_layer = lambda rows, cols, **kw: pl.BlockSpec((None, rows, cols), lambda i: (layer, 0, 0), **kw)
    att = jax.ShapeDtypeStruct((n, ATTN_DIM), BF16)
    buf = lambda width: pltpu.VMEM((HALO + PROJ_ROWS, width), F32)
    return pl.pallas_call(
        functools.partial(_in_proj_kernel, tiles_per_seq),
        name="in_proj",
        grid=(n // PROJ_ROWS,),
        in_specs=[
            row(D_MODEL),
            pl.BlockSpec((None, None, 1, D_MODEL), lambda i: (layer, 2, 0, 0)),
            of_layer(D_MODEL, IN_PROJ_WIDTH, pipeline_mode=pl.Buffered(1)),
            of_layer(CONV_WIDTH, CONV_DIM), of_layer(POOL_DIM, POOL_DIM), of_layer(1, POOL_DIM),
        ],
        out_specs=[row(ATTN_DIM), row(ATTN_DIM), row(ATTN_DIM), row(CONV_DIM + POOL_DIM)],
        out_shape=[att, att, att, jax.ShapeDtypeStruct((n, CONV_DIM + POOL_DIM), BF16)],
        scratch_shapes=[buf(CONV_DIM), buf(POOL_DIM), buf(POOL_DIM), buf(POOL_DIM), buf(POOL_DIM)],
        compiler_params=pltpu.CompilerParams(
            dimension_semantics=("arbitrary",), vmem_limit_bytes=VMEM_LIMIT),
    )(x2d, norm_g, w_in, conv_w, pool_w_bd, pool_scale)


def _attn_scores(qm, k2):
    return lax.dot_general(qm, k2, (((1,), (1,)), ((), ())), preferred_element_type=F32)


def _attn_stage1(z, neg_tri, mask):
    neg_abs = lax.bitcast_convert_type(
        lax.bitcast_convert_type(z, jnp.uint32) | jnp.uint32(0x80000000), F32)
    soft = jnp.log(1.0 + jnp.exp(neg_abs))
    drop = jnp.maximum(z, 0.0) + soft
    log_beta = z - drop
    if mask is not None:
        drop = jnp.where(mask, drop, 0.0)
    between = jnp.dot(drop.astype(BF16), neg_tri, preferred_element_type=F32)
    return log_beta, between, drop[:, 0:1]


def _attn_stage2(stage1, v2, carry, mask):
    log_beta, between, drop0 = stage1
    a = jnp.exp(log_beta + between + carry)
    if mask is not None:
        a = jnp.where(mask, a, 0.0)
    out = jnp.dot(a.astype(BF16), v2, preferred_element_type=F32)
    return out, carry + between[:, 0:1] - drop0


def _attn_out_kernel(x_ref, q_ref, k_ref, v_ref, cp_ref, wout_ref, g_ref, o_ref,
                     qm_ref, mix_ref, acc_ref, carry_ref):
    def tile(t, _):
        _attn_tile(pl.program_id(1) * ATT_TILES + t, t * ATT_Q,
                   q_ref, k_ref, v_ref, cp_ref, qm_ref, mix_ref, acc_ref, carry_ref)
        return 0

    lax.fori_loop(0, ATT_TILES, tile, 0)
    m = jnp.dot(mix_ref[...], wout_ref[...], preferred_element_type=F32)
    o_ref[0] = x_ref[0] + _rmsnorm(m, g_ref[...])


def _attn_tile(qi, row0, q_ref, k_ref, v_ref, cp_ref, qm_ref, mix_ref, acc_ref, carry_ref):
    n_pairs = ATTN_HEADS // 2
    early, late = slice(0, EARLY_ROWS), slice(EARLY_ROWS, ATT_Q)
    n_early = 2 * EARLY_ROWS
    row = lax.broadcasted_iota(jnp.int32, (ATT_Q, ATT_K), 0)
    col = lax.broadcasted_iota(jnp.int32, (ATT_Q, ATT_K), 1)
    neg_tri = jnp.where(row > col, -1.0, 0.0).astype(BF16)

    rows_early = lax.broadcasted_iota(jnp.int32, (EARLY_ROWS, ATT_K), 0)
    rows_late = EARLY_ROWS + lax.broadcasted_iota(jnp.int32, (ATT_Q - EARLY_ROWS, ATT_K), 0)
    tile_row = jnp.concatenate([rows_early, rows_early, rows_late, rows_late], axis=0)
    causal = lax.broadcasted_iota(jnp.int32, (2 * ATT_Q, ATT_K), 1) < tile_row
    first_head = lax.broadcasted_iota(jnp.int32, (ATT_Q, LANES), 1) < ATTN_HEAD_DIM
    pair_lanes = [slice(hp * LANES, (hp + 1) * LANES) for hp in range(n_pairs)]

    for hp in range(n_pairs):
        q2 = q_ref[0, pl.ds(pl.multiple_of(row0, ATT_Q), ATT_Q), pair_lanes[hp]]
        head0 = jnp.where(first_head, q2, jnp.zeros_like(q2))
        head1 = jnp.where(first_head, jnp.zeros_like(q2), q2)
        qm_ref[hp] = jnp.concatenate([head0[early], head1[early], head0[late], head1[late]], axis=0)

    def scores(hp, blocks_back, rows):
        start = pl.multiple_of((qi - blocks_back) * ATT_K, ATT_K)
        return _attn_scores(qm_ref[hp, rows, :], k_ref[0, pl.ds(start, ATT_K), pair_lanes[hp]])

    def sweep(items):
        masks = [causal if item[3] else None for item in items]
        zs = [scores(*item[:3]) for item in items[:QK_AHEAD]]
        stage1 = {0: _attn_stage1(zs[0], neg_tri, masks[0])}
        for n, (hp, blocks_back, rows, is_diagonal) in enumerate(items):
            if n + QK_AHEAD < len(items):
                zs.append(scores(*items[n + QK_AHEAD][:3]))
            if n + 1 < len(items):
                stage1[n + 1] = _attn_stage1(zs[n + 1], neg_tri, masks[n + 1])
            start = pl.multiple_of((qi - blocks_back) * ATT_K, ATT_K)
            v2 = v_ref[0, pl.ds(start, ATT_K), pair_lanes[hp]]
            if is_diagonal:
                out, carry = _attn_stage2(stage1.pop(n), v2, jnp.zeros((2 * ATT_Q, 1), F32), causal)
                acc_ref[hp, rows, :] = out
            else:
                out, carry = _attn_stage2(stage1.pop(n), v2, carry_ref[hp, rows, :], None)
                acc_ref[hp, rows, :] += out
            carry_ref[hp, rows, :] = carry

    everything, early_rows, late_rows = slice(0, 2 * ATT_Q), slice(0, n_early), slice(n_early, 2 * ATT_Q)

    def leading_blocks(with_previous):
        items = [(hp, 0, everything, True) for hp in range(n_pairs)]
        if with_previous:
            items += [(hp, 1, early_rows, False) for hp in range(n_pairs)]
        sweep(items)

    pl.when(qi == 0)(lambda: leading_blocks(False))
    pl.when(qi > 0)(lambda: leading_blocks(True))

    def weight_left(rows):
        return jnp.max(carry_ref[:, rows, :]) > UNDERFLOW_LOG

    def accumulate(blocks_back, rows):
        sweep([(hp, blocks_back, rows, False) for hp in range(n_pairs)])

    late_left = weight_left(late_rows)
    early_left = weight_left(early_rows)
    pl.when(jnp.logical_and(qi > 0, late_left))(lambda: accumulate(1, late_rows))

    def body(state):
        blocks_back, _ = state
        accumulate(blocks_back, everything)
        return blocks_back + 1, weight_left(everything)

    lax.while_loop(lambda state: jnp.logical_and(state[0] <= qi, state[1]), body,
                   (jnp.int32(2), jnp.logical_or(early_left, late_left)))

    for hp in range(n_pairs):
        for rows, first, second in ((early, slice(0, EARLY_ROWS), slice(EARLY_ROWS, n_early)),
                                    (late, slice(n_early, n_early + ATT_Q - EARLY_ROWS),
                                     slice(n_early + ATT_Q - EARLY_ROWS, 2 * ATT_Q))):
            head0_lanes = lax.broadcasted_iota(jnp.int32, (rows.stop - rows.start, LANES), 1) < ATTN_HEAD_DIM
            out_rows = pl.ds(pl.multiple_of(row0 + rows.start, math.gcd(ATT_Q, EARLY_ROWS)),
                             rows.stop - rows.start)
            mix_ref[out_rows, pair_lanes[hp]] = jnp.where(
                head0_lanes, acc_ref[hp, first, :], acc_ref[hp, second, :])
    tile_rows = pl.ds(pl.multiple_of(row0, ATT_Q), ATT_Q)
    mix_ref[tile_rows, ATTN_DIM:] = cp_ref[0, tile_rows, :].astype(F32)


def _attn_out(x3d, q, k, v, cp, w_out, norm_g, layer):
    b, s, _ = x3d.shape
    tile = lambda width: pl.BlockSpec((1, ATT_TILES * ATT_Q, width), lambda bi, qi: (bi, qi, 0))
    seq = lambda width: pl.BlockSpec((1, s, width), lambda bi, qi: (bi, 0, 0))
    return pl.pallas_call(
        _attn_out_kernel,
        name="attn_out",
        grid=(b, s // (ATT_TILES * ATT_Q)),
        in_specs=[
            tile(D_MODEL), tile(ATTN_DIM), seq(ATTN_DIM), seq(ATTN_DIM), tile(CONV_DIM + POOL_DIM),
            pl.BlockSpec((None, D_MODEL, D_MODEL), lambda bi, qi: (layer, 0, 0)),
            pl.BlockSpec((None, None, 1, D_MODEL), lambda bi, qi: (layer, 3, 0, 0)),
        ],
        out_specs=tile(D_MODEL),
        out_shape=jax.ShapeDtypeStruct(x3d.shape, F32),
        scratch_shapes=[
            pltpu.VMEM((ATTN_HEADS // 2, 2 * ATT_Q, LANES), BF16),
            pltpu.VMEM((ATT_TILES * ATT_Q, D_MODEL), F32),
            pltpu.VMEM((ATTN_HEADS // 2, 2 * ATT_Q, LANES), F32),
            pltpu.VMEM((ATTN_HEADS // 2, 2 * ATT_Q, 1), F32),
        ],
        compiler_params=pltpu.CompilerParams(
            dimension_semantics=("arbitrary", "arbitrary"), vmem_limit_bytes=VMEM_LIMIT),
    )(x3d, q, k, v, cp, w_out, norm_g)


def _block_diag(pool_w):
    depth, groups, gd, _ = pool_w.shape
    eye = jnp.eye(groups, dtype=pool_w.dtype)
    return jnp.einsum("lgij,gh->lgihj", pool_w, eye).reshape(depth, groups * gd, groups * gd)


def kernel(x, norm_g, ffn_w_gate, ffn_w_up, ffn_w_down, w_in, conv_w, pool_w, pool_scale, w_out):
    b, s, d = x.shape
    assert d == D_MODEL and s % PROJ_ROWS == 0 and s % (ATT_TILES * ATT_Q) == 0 and (b * s) % FFN_ROWS == 0
    depth = norm_g.shape[0]
    gains = norm_g.reshape(depth, 6, 1, D_MODEL)
    pool_w_bd = _block_diag(pool_w)
    pool_scale = pool_scale.reshape(depth, 1, POOL_DIM)
    to3d = lambda t: t.reshape(b, s, t.shape[-1])
    x2d = x.reshape(b * s, d)
    for l in range(depth):
        x2d = _ffn(x2d, gains, ffn_w_gate, ffn_w_up, ffn_w_down, l, 0)
        q, k, v, cp = _in_proj(x2d, s, gains, w_in, conv_w, pool_w_bd, pool_scale, l)
        x2d = _attn_out(to3d(x2d), to3d(q), to3d(k), to3d(v), to3d(cp), w_out, gains, l).reshape(b * s, d)
        x2d = _ffn(x2d, gains, ffn_w_gate, ffn_w_up, ffn_w_down, l, 1)
    return x2d.reshape(b, s, d)
```

```python
import functools
import math

import jax
import jax.numpy as jnp
from jax import lax
from jax.experimental import pallas as pl
from jax.experimental.pallas import tpu as pltpu

F32 = jnp.float32
BF16 = jnp.bfloat16

D_MODEL = 1024
D_FF = 2816
ATTN_HEADS = 8
ATTN_HEAD_DIM = 64
ATTN_DIM = ATTN_HEADS * ATTN_HEAD_DIM
CONV_DIM = 256
CONV_WIDTH = 3
POOL_WINDOWS = (2, 4, 8, 16)
POOL_DIM = 256
POOL_GD = POOL_DIM // len(POOL_WINDOWS)
IN_PROJ_WIDTH = 3 * ATTN_DIM + 3 * CONV_DIM + POOL_DIM
RMS_EPS = 1e-6

LANES = 128
FF_CHUNK = 256
N_FF_CHUNKS = D_FF // FF_CHUNK
FFN_ROWS = 1024
FFN_GROUP_ROWS = 512
PROJ_ROWS = 1024
PROJ_GROUP_ROWS = 512
HALO = 32
ATT_Q = 256
ATT_K = 256
ATT_TILES = 4
EARLY_ROWS = 176
OUT_GROUP_ROWS = 512
QK_AHEAD = 2
UNDERFLOW_LOG = -105.0
V7X_VMEM_BYTES = 64 * 1024 * 1024
VMEM_LIMIT = V7X_VMEM_BYTES * 7 // 8
FFN_VMEM_LIMIT = V7X_VMEM_BYTES * 15 // 16


def _rmsnorm(x, g):
    ms = jnp.mean(x * x, axis=-1, keepdims=True)
    return x * lax.rsqrt(ms + RMS_EPS) * g


def _ffn_kernel(x_ref, gpre_ref, gpost_ref, wg_ref, wu_ref, wd_ref, o_ref):
    groups = [slice(r, r + FFN_GROUP_ROWS) for r in range(0, FFN_ROWS, FFN_GROUP_ROWS)]
    hs = [_rmsnorm(x_ref[rows, :], gpre_ref[...]) for rows in groups]
    accs = [jnp.zeros((FFN_GROUP_ROWS, D_MODEL), F32) for _ in groups]
    for c in range(N_FF_CHUNKS):
        cols = slice(c * FF_CHUNK, (c + 1) * FF_CHUNK)
        for i, h in enumerate(hs):
            g = jnp.dot(h, wg_ref[:, cols], preferred_element_type=F32)
            u = jnp.dot(h, wu_ref[:, cols], preferred_element_type=F32)
            a = g * jax.nn.sigmoid(g) * u
            accs[i] = accs[i] + jnp.dot(a, wd_ref[cols, :], preferred_element_type=F32)
    for rows, acc in zip(groups, accs):
        o_ref[rows, :] = x_ref[rows, :] + 0.5 * _rmsnorm(acc, gpost_ref[...])


def _ffn(x2d, norm_g, w_gate, w_up, w_down, layer, half):
    n = x2d.shape[0]
    row_spec = pl.BlockSpec((FFN_ROWS, D_MODEL), lambda i: (i, 0))
    gain = lambda which: pl.BlockSpec((None, None, 1, D_MODEL), lambda i: (layer, which, 0, 0))
    resident = lambda rows, cols: pl.BlockSpec(
        (None, None, rows, cols), lambda i: (layer, half, 0, 0), pipeline_mode=pl.Buffered(1))
    pre, post = (0, 1) if half == 0 else (4, 5)
    return pl.pallas_call(
        _ffn_kernel,
        name="ffn",
        grid=(n // FFN_ROWS,),
        in_specs=[
            row_spec, gain(pre), gain(post),
            resident(D_MODEL, D_FF), resident(D_MODEL, D_FF), resident(D_FF, D_MODEL),
        ],
        out_specs=row_spec,
        out_shape=jax.ShapeDtypeStruct(x2d.shape, F32),
        compiler_params=pltpu.CompilerParams(
            dimension_semantics=("arbitrary",), vmem_limit_bytes=FFN_VMEM_LIMIT),
    )(x2d, norm_g, norm_g, w_gate, w_up, w_down)


def _in_proj_kernel(tiles_per_seq, x_ref, g_ref, w_ref, convw_ref, poolw_ref, pscale_ref,
                    q_ref, k_ref, v_ref, cp_ref, ubuf, pbuf, s2buf, s4buf, s8buf):
    tile_in_seq = pl.program_id(0) % tiles_per_seq

    @pl.when(tile_in_seq == 0)
    def _():
        ubuf[0:HALO, :] = jnp.zeros((HALO, CONV_DIM), F32)
        pbuf[0:HALO, :] = jnp.zeros((HALO, POOL_DIM), F32)

    @pl.when(tile_in_seq != 0)
    def _():
        ubuf[0:HALO, :] = ubuf[PROJ_ROWS:PROJ_ROWS + HALO, :]
        pbuf[0:HALO, :] = pbuf[PROJ_ROWS:PROJ_ROWS + HALO, :]

    starts = list(range(0, PROJ_ROWS, PROJ_GROUP_ROWS))
    mix_cols = slice(3 * ATTN_DIM, IN_PROJ_WIDTH)
    h = _rmsnorm(x_ref[0:PROJ_GROUP_ROWS, :], g_ref[...])
    for gi, r0 in enumerate(starts):
        out_rows = slice(r0, r0 + PROJ_GROUP_ROWS)
        mix_in = jnp.dot(h, w_ref[:, mix_cols], preferred_element_type=F32)
        q = jnp.dot(h, w_ref[:, 0:ATTN_DIM], preferred_element_type=F32)
        q_ref[out_rows, :] = (q * (1.0 / math.sqrt(ATTN_HEAD_DIM))).astype(BF16)
        _mix_rows(r0, tile_in_seq * PROJ_ROWS + r0, mix_in, convw_ref, poolw_ref, pscale_ref,
                  cp_ref, ubuf, pbuf, s2buf, s4buf, s8buf)
        k_ref[out_rows, :] = jnp.dot(h, w_ref[:, ATTN_DIM:2 * ATTN_DIM],
                                     preferred_element_type=F32).astype(BF16)
        h_next = None
        if gi + 1 < len(starts):
            nxt = starts[gi + 1]
            h_next = _rmsnorm(x_ref[nxt:nxt + PROJ_GROUP_ROWS, :], g_ref[...])
        v_ref[out_rows, :] = jnp.dot(h, w_ref[:, 2 * ATTN_DIM:3 * ATTN_DIM],
                                     preferred_element_type=F32).astype(BF16)
        h = h_next


def _mix_rows(r0, seq_pos, mix_in, convw_ref, poolw_ref, pscale_ref,
              cp_ref, ubuf, pbuf, s2buf, s4buf, s8buf):
    rows = PROJ_GROUP_ROWS
    out_rows = slice(r0, r0 + rows)
    gate_b = mix_in[:, 0:CONV_DIM]
    gate_c = mix_in[:, CONV_DIM:2 * CONV_DIM]
    conv_h = mix_in[:, 2 * CONV_DIM:3 * CONV_DIM]
    p = mix_in[:, 3 * CONV_DIM:]

    tok = r0 + HALO
    end = tok + rows
    ubuf[tok:end, :] = gate_c * conv_h
    cw = convw_ref[...]
    y = (cw[0:1, :] * ubuf[tok - 2:end - 2, :]
         + cw[1:2, :] * ubuf[tok - 1:end - 1, :]
         + cw[2:3, :] * ubuf[tok:end, :])
    cp_ref[out_rows, 0:CONV_DIM] = (gate_b * y).astype(BF16)

    pbuf[tok:end, :] = p
    s2buf[r0 + 8:end, :] = pbuf[r0 + 8:end, :] + pbuf[r0 + 7:end - 1, :]
    s4buf[r0 + 16:end, :] = s2buf[r0 + 16:end, :] + s2buf[r0 + 14:end - 2, :]
    s8buf[r0 + 24:end, :] = s4buf[r0 + 24:end, :] + s4buf[r0 + 20:end - 4, :]
    s16 = s8buf[tok:end, :] + s8buf[tok - 8:end - 8, :]
    group = lax.broadcasted_iota(jnp.int32, (rows, POOL_DIM), 1) // POOL_GD
    win_sum = jnp.where(group == 0, s2buf[tok:end, :],
                        jnp.where(group == 1, s4buf[tok:end, :],
                                  jnp.where(group == 2, s8buf[tok:end, :], s16)))
    window = jnp.where(group == 0, POOL_WINDOWS[0],
                       jnp.where(group == 1, POOL_WINDOWS[1],
                                 jnp.where(group == 2, POOL_WINDOWS[2], POOL_WINDOWS[3])))
    pos = seq_pos + lax.broadcasted_iota(jnp.int32, (rows, POOL_DIM), 0)
    count = jnp.minimum(pos + 1, window).astype(F32)
    d = win_sum / count - p
    pooled = jnp.dot(d, poolw_ref[...], preferred_element_type=F32)
    cp_ref[out_rows, CONV_DIM:] = (pooled * pscale_ref[...]).astype(BF16)


def _in_proj(x2d, seq, norm_g, w_in, conv_w, pool_w_bd, pool_scale, layer):
    n = x2d.shape[0]
    tiles_per_seq = seq // PROJ_ROWS
    row = lambda width: pl.BlockSpec((PROJ_ROWS, width), lambda i: (i, 0))
    of_layer = lambda rows, cols, **kw: pl.BlockSpec((None, rows, cols), lambda i: (layer, 0, 0), **kw)
    att = jax.ShapeDtypeStruct((n, ATTN_DIM), BF16)
    buf = lambda width: pltpu.VMEM((HALO + PROJ_ROWS, width), F32)
    return pl.pallas_call(
        functools.partial(_in_proj_kernel, tiles_per_seq),
        name="in_proj",
        grid=(n // PROJ_ROWS,),
        in_specs=[
            row(D_MODEL),
            pl.BlockSpec((None, None, 1, D_MODEL), lambda i: (layer, 2, 0, 0)),
            of_layer(D_MODEL, IN_PROJ_WIDTH, pipeline_mode=pl.Buffered(1)),
            of_layer(CONV_WIDTH, CONV_DIM), of_layer(POOL_DIM, POOL_DIM), of_layer(1, POOL_DIM),
        ],
        out_specs=[row(ATTN_DIM), row(ATTN_DIM), row(ATTN_DIM), row(CONV_DIM + POOL_DIM)],
        out_shape=[att, att, att, jax.ShapeDtypeStruct((n, CONV_DIM + POOL_DIM), BF16)],
        scratch_shapes=[buf(CONV_DIM), buf(POOL_DIM), buf(POOL_DIM), buf(POOL_DIM), buf(POOL_DIM)],
        compiler_params=pltpu.CompilerParams(
            dimension_semantics=("arbitrary",), vmem_limit_bytes=VMEM_LIMIT),
    )(x2d, norm_g, w_in, conv_w, pool_w_bd, pool_scale)


def _attn_scores(qm, k2):
    return lax.dot_general(qm, k2, (((1,), (1,)), ((), ())), preferred_element_type=F32)


def _attn_stage1(z, neg_tri, mask):
    neg_abs = lax.bitcast_convert_type(
        lax.bitcast_convert_type(z, jnp.uint32) | jnp.uint32(0x80000000), F32)
    soft = jnp.log(1.0 + jnp.exp(neg_abs))
    drop = jnp.maximum(z, 0.0) + soft
    log_beta = z - drop
    if mask is not None:
        drop = jnp.where(mask, drop, 0.0)
    between = jnp.dot(drop.astype(BF16), neg_tri, preferred_element_type=F32)
    return log_beta, between, drop[:, 0:1]


def _attn_stage2(stage1, v2, carry, mask):
    log_beta, between, drop0 = stage1
    a = jnp.exp(log_beta + between + carry)
    if mask is not None:
        a = jnp.where(mask, a, 0.0)
    out = jnp.dot(a.astype(BF16), v2, preferred_element_type=F32)
    return out, carry + between[:, 0:1] - drop0


def _attn_out_kernel(x_ref, q_ref, k_ref, v_ref, cp_ref, wout_ref, g_ref, o_ref,
                     qm_ref, mix_ref, acc_ref, carry_ref):
    def tile(t, _):
        _attn_tile(pl.program_id(1) * ATT_TILES + t, t * ATT_Q,
                   q_ref, k_ref, v_ref, cp_ref, qm_ref, mix_ref, acc_ref, carry_ref)
        return 0

    lax.fori_loop(0, ATT_TILES, tile, 0)

    def finish(rows, m):
        o_ref[0, rows, :] = x_ref[0, rows, :] + _rmsnorm(m, g_ref[...])

    pending = None
    for r0 in range(0, ATT_TILES * ATT_Q, OUT_GROUP_ROWS):
        rows = slice(r0, r0 + OUT_GROUP_ROWS)
        m = jnp.dot(mix_ref[rows, :], wout_ref[...], preferred_element_type=F32)
        if pending is not None:
            finish(*pending)
        pending = (rows, m)
    finish(*pending)


def _attn_tile(qi, row0, q_ref, k_ref, v_ref, cp_ref, qm_ref, mix_ref, acc_ref, carry_ref):
    n_pairs = ATTN_HEADS // 2
    early, late = slice(0, EARLY_ROWS), slice(EARLY_ROWS, ATT_Q)
    n_early = 2 * EARLY_ROWS
    row = lax.broadcasted_iota(jnp.int32, (ATT_Q, ATT_K), 0)
    col = lax.broadcasted_iota(jnp.int32, (ATT_Q, ATT_K), 1)
    neg_tri = jnp.where(row > col, -1.0, 0.0).astype(BF16)

    rows_early = lax.broadcasted_iota(jnp.int32, (EARLY_ROWS, ATT_K), 0)
    rows_late = EARLY_ROWS + lax.broadcasted_iota(jnp.int32, (ATT_Q - EARLY_ROWS, ATT_K), 0)
    tile_row = jnp.concatenate([rows_early, rows_early, rows_late, rows_late], axis=0)
    causal = lax.broadcasted_iota(jnp.int32, (2 * ATT_Q, ATT_K), 1) < tile_row
    first_head = lax.broadcasted_iota(jnp.int32, (ATT_Q, LANES), 1) < ATTN_HEAD_DIM
    pair_lanes = [slice(hp * LANES, (hp + 1) * LANES) for hp in range(n_pairs)]

    for hp in range(n_pairs):
        q2 = q_ref[0, pl.ds(pl.multiple_of(row0, ATT_Q), ATT_Q), pair_lanes[hp]]
        head0 = jnp.where(first_head, q2, jnp.zeros_like(q2))
        head1 = jnp.where(first_head, jnp.zeros_like(q2), q2)
        qm_ref[hp] = jnp.concatenate([head0[early], head1[early], head0[late], head1[late]], axis=0)

    def scores(hp, blocks_back, rows):
        start = pl.multiple_of((qi - blocks_back) * ATT_K, ATT_K)
        return _attn_scores(qm_ref[hp, rows, :], k_ref[0, pl.ds(start, ATT_K), pair_lanes[hp]])

    def sweep(items):
        masks = [causal if item[3] else None for item in items]
        zs = [scores(*item[:3]) for item in items[:QK_AHEAD]]
        stage1 = {0: _attn_stage1(zs[0], neg_tri, masks[0])}
        for n, (hp, blocks_back, rows, is_diagonal) in enumerate(items):
            if n + QK_AHEAD < len(items):
                zs.append(scores(*items[n + QK_AHEAD][:3]))
            if n + 1 < len(items):
                stage1[n + 1] = _attn_stage1(zs[n + 1], neg_tri, masks[n + 1])
            start = pl.multiple_of((qi - blocks_back) * ATT_K, ATT_K)
            v2 = v_ref[0, pl.ds(start, ATT_K), pair_lanes[hp]]
            if is_diagonal:
                out, carry = _attn_stage2(stage1.pop(n), v2, jnp.zeros((2 * ATT_Q, 1), F32), causal)
                acc_ref[hp, rows, :] = out
            else:
                out, carry = _attn_stage2(stage1.pop(n), v2, carry_ref[hp, rows, :], None)
                acc_ref[hp, rows, :] += out
            carry_ref[hp, rows, :] = carry

    everything, early_rows, late_rows = slice(0, 2 * ATT_Q), slice(0, n_early), slice(n_early, 2 * ATT_Q)

    def leading_blocks(with_previous):
        items = [(hp, 0, everything, True) for hp in range(n_pairs)]
        if with_previous:
            items += [(hp, 1, early_rows, False) for hp in range(n_pairs)]
        sweep(items)

    pl.when(qi == 0)(lambda: leading_blocks(False))
    pl.when(qi > 0)(lambda: leading_blocks(True))

    def weight_left(rows):
        return jnp.max(carry_ref[:, rows, :]) > UNDERFLOW_LOG

    def accumulate(blocks_back, rows):
        sweep([(hp, blocks_back, rows, False) for hp in range(n_pairs)])

    late_left = weight_left(late_rows)
    early_left = weight_left(early_rows)
    pl.when(jnp.logical_and(qi > 0, late_left))(lambda: accumulate(1, late_rows))

    def body(state):
        blocks_back, _ = state
        accumulate(blocks_back, everything)
        return blocks_back + 1, weight_left(everything)

    lax.while_loop(lambda state: jnp.logical_and(state[0] <= qi, state[1]), body,
                   (jnp.int32(2), jnp.logical_or(early_left, late_left)))

    for hp in range(n_pairs):
        for rows, first, second in ((early, slice(0, EARLY_ROWS), slice(EARLY_ROWS, n_early)),
                                    (late, slice(n_early, n_early + ATT_Q - EARLY_ROWS),
                                     slice(n_early + ATT_Q - EARLY_ROWS, 2 * ATT_Q))):
            head0_lanes = lax.broadcasted_iota(jnp.int32, (rows.stop - rows.start, LANES), 1) < ATTN_HEAD_DIM
            out_rows = pl.ds(pl.multiple_of(row0 + rows.start, math.gcd(ATT_Q, EARLY_ROWS)),
                             rows.stop - rows.start)
            mix_ref[out_rows, pair_lanes[hp]] = jnp.where(
                head0_lanes, acc_ref[hp, first, :], acc_ref[hp, second, :])
    tile_rows = pl.ds(pl.multiple_of(row0, ATT_Q), ATT_Q)
    mix_ref[tile_rows, ATTN_DIM:] = cp_ref[0, tile_rows, :].astype(F32)


def _attn_out(x3d, q, k, v, cp, w_out, norm_g, layer):
    b, s, _ = x3d.shape
    tile = lambda width: pl.BlockSpec((1, ATT_TILES * ATT_Q, width), lambda bi, qi: (bi, qi, 0))
    seq = lambda width: pl.BlockSpec((1, s, width), lambda bi, qi: (bi, 0, 0))
    return pl.pallas_call(
        _attn_out_kernel,
        name="attn_out",
        grid=(b, s // (ATT_TILES * ATT_Q)),
        in_specs=[
            tile(D_MODEL), tile(ATTN_DIM), seq(ATTN_DIM), seq(ATTN_DIM), tile(CONV_DIM + POOL_DIM),
            pl.BlockSpec((None, D_MODEL, D_MODEL), lambda bi, qi: (layer, 0, 0)),
            pl.BlockSpec((None, None, 1, D_MODEL), lambda bi, qi: (layer, 3, 0, 0)),
        ],
        out_specs=tile(D_MODEL),
        out_shape=jax.ShapeDtypeStruct(x3d.shape, F32),
        scratch_shapes=[
            pltpu.VMEM((ATTN_HEADS // 2, 2 * ATT_Q, LANES), BF16),
            pltpu.VMEM((ATT_TILES * ATT_Q, D_MODEL), F32),
            pltpu.VMEM((ATTN_HEADS // 2, 2 * ATT_Q, LANES), F32),
            pltpu.VMEM((ATTN_HEADS // 2, 2 * ATT_Q, 1), F32),
        ],
        compiler_params=pltpu.CompilerParams(
            dimension_semantics=("arbitrary", "arbitrary"), vmem_limit_bytes=VMEM_LIMIT),
    )(x3d, q, k, v, cp, w_out, norm_g)


def _block_diag(pool_w):
    depth, groups, gd, _ = pool_w.shape
    eye = jnp.eye(groups, dtype=pool_w.dtype)
    return jnp.einsum("lgij,gh->lgihj", pool_w, eye).reshape(depth, groups * gd, groups * gd)


def kernel(x, norm_g, ffn_w_gate, ffn_w_up, ffn_w_down, w_in, conv_w, pool_w, pool_scale, w_out):
    b, s, d = x.shape
    assert d == D_MODEL and s % PROJ_ROWS == 0 and s % (ATT_TILES * ATT_Q) == 0 and (b * s) % FFN_ROWS == 0
    depth = norm_g.shape[0]
    gains = norm_g.reshape(depth, 6, 1, D_MODEL)
    pool_w_bd = _block_diag(pool_w)
    pool_scale = pool_scale.reshape(depth, 1, POOL_DIM)
    to3d = lambda t: t.reshape(b, s, t.shape[-1])
    x2d = x.reshape(b * s, d)
    for l in range(depth):
        x2d = _ffn(x2d, gains, ffn_w_gate, ffn_w_up, ffn_w_down, l, 0)
        q, k, v, cp = _in_proj(x2d, s, gains, w_in, conv_w, pool_w_bd, pool_scale, l)
        x2d = _attn_out(to3d(x2d), to3d(q), to3d(k), to3d(v), to3d(cp), w_out, gains, l).reshape(b * s, d)
        x2d = _ffn(x2d, gains, ffn_w_gate, ffn_w_up, ffn_w_down, l, 1)
    return x2d.reshape(b, s, d)
```

```python
import functools
import math

import jax
import jax.numpy as jnp
from jax import lax
from jax.experimental import pallas as pl
from jax.experimental.pallas import tpu as pltpu

F32 = jnp.float32
BF16 = jnp.bfloat16

D_MODEL = 1024
D_FF = 2816
ATTN_HEADS = 8
ATTN_HEAD_DIM = 64
ATTN_DIM = ATTN_HEADS * ATTN_HEAD_DIM
CONV_DIM = 256
CONV_WIDTH = 3
POOL_WINDOWS = (2, 4, 8, 16)
POOL_DIM = 256
POOL_GD = POOL_DIM // len(POOL_WINDOWS)
IN_PROJ_WIDTH = 3 * ATTN_DIM + 3 * CONV_DIM + POOL_DIM
RMS_EPS = 1e-6
MIXED_WIDTH = 3 * ATTN_DIM + CONV_DIM + POOL_DIM

LANES = 128
FF_CHUNK = 256
N_FF_CHUNKS = D_FF // FF_CHUNK
FFN_ROWS = 1024
FFN_GROUP_ROWS = 512
PROJ_ROWS = 1024
PROJ_GROUP_ROWS = 512
HALO = 32
ATT_Q = 256
ATT_K = 256
ATT_TILES = 4
EARLY_ROWS = 176
QK_AHEAD = 2
UNDERFLOW_LOG = -105.0
V7X_VMEM_BYTES = 64 * 1024 * 1024
VMEM_LIMIT = V7X_VMEM_BYTES * 7 // 8
FFN_VMEM_LIMIT = V7X_VMEM_BYTES * 15 // 16


def _rmsnorm(x, g):
    ms = jnp.mean(x * x, axis=-1, keepdims=True)
    return x * lax.rsqrt(ms + RMS_EPS) * g


def _ffn_kernel(x_ref, gpre_ref, gpost_ref, wg_ref, wu_ref, wd_ref, o_ref):
    groups = [slice(r, r + FFN_GROUP_ROWS) for r in range(0, FFN_ROWS, FFN_GROUP_ROWS)]
    hs = [_rmsnorm(x_ref[rows, :], gpre_ref[...]) for rows in groups]
    accs = [jnp.zeros((FFN_GROUP_ROWS, D_MODEL), F32) for _ in groups]
    for c in range(N_FF_CHUNKS):
        cols = slice(c * FF_CHUNK, (c + 1) * FF_CHUNK)
        for i, h in enumerate(hs):
            g = jnp.dot(h, wg_ref[:, cols], preferred_element_type=F32)
            u = jnp.dot(h, wu_ref[:, cols], preferred_element_type=F32)
            a = g * jax.nn.sigmoid(g) * u
            accs[i] = accs[i] + jnp.dot(a, wd_ref[cols, :], preferred_element_type=F32)
    for rows, acc in zip(groups, accs):
        o_ref[rows, :] = x_ref[rows, :] + 0.5 * _rmsnorm(acc, gpost_ref[...])


def _ffn(x2d, norm_g, w_gate, w_up, w_down, layer, half):
    n = x2d.shape[0]
    row_spec = pl.BlockSpec((FFN_ROWS, D_MODEL), lambda i: (i, 0))
    gain = lambda which: pl.BlockSpec((None, None, 1, D_MODEL), lambda i: (layer, which, 0, 0))
    resident = lambda rows, cols: pl.BlockSpec(
        (None, None, rows, cols), lambda i: (layer, half, 0, 0), pipeline_mode=pl.Buffered(1))
    pre, post = (0, 1) if half == 0 else (4, 5)
    return pl.pallas_call(
        _ffn_kernel,
        name="ffn",
        grid=(n // FFN_ROWS,),
        in_specs=[
            row_spec, gain(pre), gain(post),
            resident(D_MODEL, D_FF), resident(D_MODEL, D_FF), resident(D_FF, D_MODEL),
        ],
        out_specs=row_spec,
        out_shape=jax.ShapeDtypeStruct(x2d.shape, F32),
        compiler_params=pltpu.CompilerParams(
            dimension_semantics=("arbitrary",), vmem_limit_bytes=FFN_VMEM_LIMIT),
    )(x2d, norm_g, norm_g, w_gate, w_up, w_down)


def _in_proj_kernel(tiles_per_seq, x_ref, g_ref, w_ref, convw_ref, poolw_ref, pscale_ref,
                    mixed_ref, ubuf, pbuf, s2buf, s4buf, s8buf):
    q_ref, k_ref, v_ref, cp_ref = (mixed_ref.at[:, i * ATTN_DIM:(i + 1) * ATTN_DIM] for i in range(4))
    tile_in_seq = pl.program_id(0) % tiles_per_seq

    @pl.when(tile_in_seq == 0)
    def _():
        ubuf[0:HALO, :] = jnp.zeros((HALO, CONV_DIM), F32)
        pbuf[0:HALO, :] = jnp.zeros((HALO, POOL_DIM), F32)

    @pl.when(tile_in_seq != 0)
    def _():
        ubuf[0:HALO, :] = ubuf[PROJ_ROWS:PROJ_ROWS + HALO, :]
        pbuf[0:HALO, :] = pbuf[PROJ_ROWS:PROJ_ROWS + HALO, :]

    starts = list(range(0, PROJ_ROWS, PROJ_GROUP_ROWS))
    mix_cols = slice(3 * ATTN_DIM, IN_PROJ_WIDTH)
    h = _rmsnorm(x_ref[0:PROJ_GROUP_ROWS, :], g_ref[...])
    for gi, r0 in enumerate(starts):
        out_rows = slice(r0, r0 + PROJ_GROUP_ROWS)
        mix_in = jnp.dot(h, w_ref[:, mix_cols], preferred_element_type=F32)
        q = jnp.dot(h, w_ref[:, 0:ATTN_DIM], preferred_element_type=F32)
        q_ref[out_rows, :] = (q * (1.0 / math.sqrt(ATTN_HEAD_DIM))).astype(BF16)
        _mix_rows(r0, tile_in_seq * PROJ_ROWS + r0, mix_in, convw_ref, poolw_ref, pscale_ref,
                  cp_ref, ubuf, pbuf, s2buf, s4buf, s8buf)
        k_ref[out_rows, :] = jnp.dot(h, w_ref[:, ATTN_DIM:2 * ATTN_DIM],
                                     preferred_element_type=F32).astype(BF16)
        h_next = None
        if gi + 1 < len(starts):
            nxt = starts[gi + 1]
            h_next = _rmsnorm(x_ref[nxt:nxt + PROJ_GROUP_ROWS, :], g_ref[...])
        v_ref[out_rows, :] = jnp.dot(h, w_ref[:, 2 * ATTN_DIM:3 * ATTN_DIM],
                                     preferred_element_type=F32).astype(BF16)
        h = h_next


def _mix_rows(r0, seq_pos, mix_in, convw_ref, poolw_ref, pscale_ref,
              cp_ref, ubuf, pbuf, s2buf, s4buf, s8buf):
    rows = PROJ_GROUP_ROWS
    out_rows = slice(r0, r0 + rows)
    gate_b = mix_in[:, 0:CONV_DIM]
    gate_c = mix_in[:, CONV_DIM:2 * CONV_DIM]
    conv_h = mix_in[:, 2 * CONV_DIM:3 * CONV_DIM]
    p = mix_in[:, 3 * CONV_DIM:]

    tok = r0 + HALO
    end = tok + rows
    ubuf[tok:end, :] = gate_c * conv_h
    cw = convw_ref[...]
    y = (cw[0:1, :] * ubuf[tok - 2:end - 2, :]
         + cw[1:2, :] * ubuf[tok - 1:end - 1, :]
         + cw[2:3, :] * ubuf[tok:end, :])
    cp_ref[out_rows, 0:CONV_DIM] = (gate_b * y).astype(BF16)

    pbuf[tok:end, :] = p
    s2buf[r0 + 8:end, :] = pbuf[r0 + 8:end, :] + pbuf[r0 + 7:end - 1, :]
    s4buf[r0 + 16:end, :] = s2buf[r0 + 16:end, :] + s2buf[r0 + 14:end - 2, :]
    s8buf[r0 + 24:end, :] = s4buf[r0 + 24:end, :] + s4buf[r0 + 20:end - 4, :]
    s16 = s8buf[tok:end, :] + s8buf[tok - 8:end - 8, :]
    group = lax.broadcasted_iota(jnp.int32, (rows, POOL_DIM), 1) // POOL_GD
    win_sum = jnp.where(group == 0, s2buf[tok:end, :],
                        jnp.where(group == 1, s4buf[tok:end, :],
                                  jnp.where(group == 2, s8buf[tok:end, :], s16)))
    window = jnp.where(group == 0, POOL_WINDOWS[0],
                       jnp.where(group == 1, POOL_WINDOWS[1],
                                 jnp.where(group == 2, POOL_WINDOWS[2], POOL_WINDOWS[3])))
    pos = seq_pos + lax.broadcasted_iota(jnp.int32, (rows, POOL_DIM), 0)
    count = jnp.minimum(pos + 1, window).astype(F32)
    d = win_sum / count - p
    pooled = jnp.dot(d, poolw_ref[...], preferred_element_type=F32)
    cp_ref[out_rows, CONV_DIM:] = (pooled * pscale_ref[...]).astype(BF16)


def _in_proj(x2d, seq, norm_g, w_in, conv_w, pool_w_bd, pool_scale, layer):
    n = x2d.shape[0]
    tiles_per_seq = seq // PROJ_ROWS
    row = lambda width: pl.BlockSpec((PROJ_ROWS, width), lambda i: (i, 0))
    of_layer = lambda rows, cols, **kw: pl.BlockSpec((None, rows, cols), lambda i: (layer, 0, 0), **kw)
    buf = lambda width: pltpu.VMEM((HALO + PROJ_ROWS, width), F32)
    return pl.pallas_call(
        functools.partial(_in_proj_kernel, tiles_per_seq),
        name="in_proj",
        grid=(n // PROJ_ROWS,),
        in_specs=[
            row(D_MODEL),
            pl.BlockSpec((None, None, 1, D_MODEL), lambda i: (layer, 2, 0, 0)),
            of_layer(D_MODEL, IN_PROJ_WIDTH, pipeline_mode=pl.Buffered(1)),
            of_layer(CONV_WIDTH, CONV_DIM), of_layer(POOL_DIM, POOL_DIM), of_layer(1, POOL_DIM),
        ],
        out_specs=row(MIXED_WIDTH),
        out_shape=jax.ShapeDtypeStruct((n, MIXED_WIDTH), BF16),
        scratch_shapes=[buf(CONV_DIM), buf(POOL_DIM), buf(POOL_DIM), buf(POOL_DIM), buf(POOL_DIM)],
        compiler_params=pltpu.CompilerParams(
            dimension_semantics=("arbitrary",), vmem_limit_bytes=VMEM_LIMIT),
    )(x2d, norm_g, w_in, conv_w, pool_w_bd, pool_scale)


def _attn_scores(qm, k2):
    return lax.dot_general(qm, k2, (((1,), (1,)), ((), ())), preferred_element_type=F32)


def _attn_stage1(z, neg_tri, mask):
    neg_abs = lax.bitcast_convert_type(
        lax.bitcast_convert_type(z, jnp.uint32) | jnp.uint32(0x80000000), F32)
    soft = jnp.log(1.0 + jnp.exp(neg_abs))
    drop = jnp.maximum(z, 0.0) + soft
    log_beta = z - drop
    if mask is not None:
        drop = jnp.where(mask, drop, 0.0)
    between = jnp.dot(drop.astype(BF16), neg_tri, preferred_element_type=F32)
    return log_beta, between, drop[:, 0:1]


def _attn_stage2(stage1, v2, carry, mask):
    log_beta, between, drop0 = stage1
    a = jnp.exp(log_beta + between + carry)
    if mask is not None:
        a = jnp.where(mask, a, 0.0)
    out = jnp.dot(a.astype(BF16), v2, preferred_element_type=F32)
    return out, carry + between[:, 0:1] - drop0


def _attn_out_kernel(x_ref, q_ref, k_ref, v_ref, cp_ref, wout_ref, g_ref, o_ref,
                     qm_ref, mix_ref, acc_ref, carry_ref):
    def tile(t, _):
        _attn_tile(pl.program_id(1) * ATT_TILES + t, t * ATT_Q,
                   q_ref, k_ref, v_ref, cp_ref, qm_ref, mix_ref, acc_ref, carry_ref)
        return 0

    lax.fori_loop(0, ATT_TILES, tile, 0)
    m = jnp.dot(mix_ref[...], wout_ref[...], preferred_element_type=F32)
    o_ref[0] = x_ref[0] + _rmsnorm(m, g_ref[...])


def _attn_tile(qi, row0, q_ref, k_ref, v_ref, cp_ref, qm_ref, mix_ref, acc_ref, carry_ref):
    n_pairs = ATTN_HEADS // 2
    early, late = slice(0, EARLY_ROWS), slice(EARLY_ROWS, ATT_Q)
    n_early = 2 * EARLY_ROWS
    row = lax.broadcasted_iota(jnp.int32, (ATT_Q, ATT_K), 0)
    col = lax.broadcasted_iota(jnp.int32, (ATT_Q, ATT_K), 1)
    neg_tri = jnp.where(row > col, -1.0, 0.0).astype(BF16)

    rows_early = lax.broadcasted_iota(jnp.int32, (EARLY_ROWS, ATT_K), 0)
    rows_late = EARLY_ROWS + lax.broadcasted_iota(jnp.int32, (ATT_Q - EARLY_ROWS, ATT_K), 0)
    tile_row = jnp.concatenate([rows_early, rows_early, rows_late, rows_late], axis=0)
    causal = lax.broadcasted_iota(jnp.int32, (2 * ATT_Q, ATT_K), 1) < tile_row
    first_head = lax.broadcasted_iota(jnp.int32, (ATT_Q, LANES), 1) < ATTN_HEAD_DIM
    pair_lanes = [slice(hp * LANES, (hp + 1) * LANES) for hp in range(n_pairs)]

    for hp in range(n_pairs):
        q2 = q_ref[0, pl.ds(pl.multiple_of(row0, ATT_Q), ATT_Q), pair_lanes[hp]]
        head0 = jnp.where(first_head, q2, jnp.zeros_like(q2))
        head1 = jnp.where(first_head, jnp.zeros_like(q2), q2)
        qm_ref[hp] = jnp.concatenate([head0[early], head1[early], head0[late], head1[late]], axis=0)

    def scores(hp, blocks_back, rows):
        start = pl.multiple_of((qi - blocks_back) * ATT_K, ATT_K)
        return _attn_scores(qm_ref[hp, rows, :], k_ref[0, pl.ds(start, ATT_K), pair_lanes[hp]])

    def sweep(items):
        masks = [causal if item[3] else None for item in items]
        zs = [scores(*item[:3]) for item in items[:QK_AHEAD]]
        stage1 = {0: _attn_stage1(zs[0], neg_tri, masks[0])}
        for n, (hp, blocks_back, rows, is_diagonal) in enumerate(items):
            if n + QK_AHEAD < len(items):
                zs.append(scores(*items[n + QK_AHEAD][:3]))
            if n + 1 < len(items):
                stage1[n + 1] = _attn_stage1(zs[n + 1], neg_tri, masks[n + 1])
            start = pl.multiple_of((qi - blocks_back) * ATT_K, ATT_K)
            v2 = v_ref[0, pl.ds(start, ATT_K), pair_lanes[hp]]
            if is_diagonal:
                out, carry = _attn_stage2(stage1.pop(n), v2, jnp.zeros((2 * ATT_Q, 1), F32), causal)
                acc_ref[hp, rows, :] = out
            else:
                out, carry = _attn_stage2(stage1.pop(n), v2, carry_ref[hp, rows, :], None)
                acc_ref[hp, rows, :] += out
            carry_ref[hp, rows, :] = carry

    everything, early_rows, late_rows = slice(0, 2 * ATT_Q), slice(0, n_early), slice(n_early, 2 * ATT_Q)

    def leading_blocks(with_previous):
        items = [(hp, 0, everything, True) for hp in range(n_pairs)]
        if with_previous:
            items += [(hp, 1, early_rows, False) for hp in range(n_pairs)]
        sweep(items)

    pl.when(qi == 0)(lambda: leading_blocks(False))
    pl.when(qi > 0)(lambda: leading_blocks(True))

    def weight_left(rows):
        return jnp.max(carry_ref[:, rows, :]) > UNDERFLOW_LOG

    def accumulate(blocks_back, rows):
        sweep([(hp, blocks_back, rows, False) for hp in range(n_pairs)])

    late_left = weight_left(late_rows)
    early_left = weight_left(early_rows)
    pl.when(jnp.logical_and(qi > 0, late_left))(lambda: accumulate(1, late_rows))

    def body(state):
        blocks_back, _ = state
        accumulate(blocks_back, everything)
        return blocks_back + 1, weight_left(everything)

    lax.while_loop(lambda state: jnp.logical_and(state[0] <= qi, state[1]), body,
                   (jnp.int32(2), jnp.logical_or(early_left, late_left)))

    for hp in range(n_pairs):
        for rows, first, second in ((early, slice(0, EARLY_ROWS), slice(EARLY_ROWS, n_early)),
                                    (late, slice(n_early, n_early + ATT_Q - EARLY_ROWS),
                                     slice(n_early + ATT_Q - EARLY_ROWS, 2 * ATT_Q))):
            head0_lanes = lax.broadcasted_iota(jnp.int32, (rows.stop - rows.start, LANES), 1) < ATTN_HEAD_DIM
            out_rows = pl.ds(pl.multiple_of(row0 + rows.start, math.gcd(ATT_Q, EARLY_ROWS)),
                             rows.stop - rows.start)
            mix_ref[out_rows, pair_lanes[hp]] = jnp.where(
                head0_lanes, acc_ref[hp, first, :], acc_ref[hp, second, :])
    tile_rows = pl.ds(pl.multiple_of(row0, ATT_Q), ATT_Q)
    mix_ref[tile_rows, ATTN_DIM:] = cp_ref[0, tile_rows, :].astype(F32)


def _attn_out(x3d, mixed, w_out, norm_g, layer):
    b, s, _ = x3d.shape
    tile = lambda width, lane_block=0: pl.BlockSpec(
        (1, ATT_TILES * ATT_Q, width), lambda bi, qi: (bi, qi, lane_block))
    seq = lambda width, lane_block: pl.BlockSpec((1, s, width), lambda bi, qi: (bi, 0, lane_block))
    return pl.pallas_call(
        _attn_out_kernel,
        name="attn_out",
        grid=(b, s // (ATT_TILES * ATT_Q)),
        in_specs=[
            tile(D_MODEL), tile(ATTN_DIM, 0), seq(ATTN_DIM, 1), seq(ATTN_DIM, 2), tile(ATTN_DIM, 3),
            pl.BlockSpec((None, D_MODEL, D_MODEL), lambda bi, qi: (layer, 0, 0)),
            pl.BlockSpec((None, None, 1, D_MODEL), lambda bi, qi: (layer, 3, 0, 0)),
        ],
        out_specs=tile(D_MODEL),
        out_shape=jax.ShapeDtypeStruct(x3d.shape, F32),
        scratch_shapes=[
            pltpu.VMEM((ATTN_HEADS // 2, 2 * ATT_Q, LANES), BF16),
            pltpu.VMEM((ATT_TILES * ATT_Q, D_MODEL), F32),
            pltpu.VMEM((ATTN_HEADS // 2, 2 * ATT_Q, LANES), F32),
            pltpu.VMEM((ATTN_HEADS // 2, 2 * ATT_Q, 1), F32),
        ],
        compiler_params=pltpu.CompilerParams(
            dimension_semantics=("arbitrary", "arbitrary"), vmem_limit_bytes=VMEM_LIMIT),
    )(x3d, mixed, mixed, mixed, mixed, w_out, norm_g)


def _block_diag(pool_w):
    depth, groups, gd, _ = pool_w.shape
    eye = jnp.eye(groups, dtype=pool_w.dtype)
    return jnp.einsum("lgij,gh->lgihj", pool_w, eye).reshape(depth, groups * gd, groups * gd)


def kernel(x, norm_g, ffn_w_gate, ffn_w_up, ffn_w_down, w_in, conv_w, pool_w, pool_scale, w_out):
    b, s, d = x.shape
    assert d == D_MODEL and s % PROJ_ROWS == 0 and s % (ATT_TILES * ATT_Q) == 0 and (b * s) % FFN_ROWS == 0
    depth = norm_g.shape[0]
    gains = norm_g.reshape(depth, 6, 1, D_MODEL)
    pool_w_bd = _block_diag(pool_w)
    pool_scale = pool_scale.reshape(depth, 1, POOL_DIM)
    x2d = x.reshape(b * s, d)
    for l in range(depth):
        x2d = _ffn(x2d, gains, ffn_w_gate, ffn_w_up, ffn_w_down, l, 0)
        mixed = _in_proj(x2d, s, gains, w_in, conv_w, pool_w_bd, pool_scale, l)
        x2d = _attn_out(x2d.reshape(b, s, d), mixed.reshape(b, s, MIXED_WIDTH), w_out, gains, l).reshape(b * s, d)
        x2d = _ffn(x2d, gains, ffn_w_gate, ffn_w_up, ffn_w_down, l, 1)
    return x2d.reshape(b, s, d)
```
